```python
import math
import jax, jax.numpy as jnp
from jax import lax
import numpy as np

D_MODEL = 1024
BATCH = 32
SEQ = 2048
DEPTH = 2

D_FF = 2816
FFN_RES = 0.5
EPS = 1e-6

SSM_HEADS = 16
SSM_HEAD_DIM = 64
SSM_D = SSM_HEADS * SSM_HEAD_DIM
SSM_GROUPS = 4
SSM_STATE = 128
SSM_CONV = 4
SSM_CHUNK = 128
SSM_XBC = SSM_D + 2 * SSM_GROUPS * SSM_STATE

MLA_HEADS = 8
MLA_Q_RANK = 384
MLA_KV_RANK = 256
MLA_NOPE = 64
MLA_ROPE = 32
MLA_QK = MLA_NOPE + MLA_ROPE
MLA_V = 64
ROPE_THETA = 10000.0

SWA_Q_HEADS = 8
SWA_KV_HEADS = 2
SWA_HD = 64
SWA_WINDOW = 128

GLA_HEADS = 4
GLA_DK = 64
GLA_DV = 128
GLA_RANK = 16
GLA_TAU = 16.0
GLA_CHUNK = 64

Q_BLOCK = 128

IN_EVEN = SSM_D + SSM_XBC + SSM_HEADS + MLA_Q_RANK + MLA_KV_RANK + MLA_ROPE
OUT_EVEN = SSM_D + MLA_HEADS * MLA_V
IN_ODD = (SWA_Q_HEADS * SWA_HD + 2 * SWA_KV_HEADS * SWA_HD
          + 2 * GLA_HEADS * GLA_DK + GLA_HEADS * GLA_DV + GLA_RANK + GLA_HEADS * GLA_DV)
OUT_ODD = SWA_Q_HEADS * SWA_HD + GLA_HEADS * GLA_DV
N_EVEN = (DEPTH + 1) // 2
N_ODD = DEPTH // 2

kernel_name = "hybrid_ssd_mla_swa_gla_macaron"


def split_last(t, sizes):
    idx, acc = [], 0
    for s in sizes[:-1]:
        acc += s
        idx.append(acc)
    return jnp.split(t, idx, axis=-1)


def rms_norm(x, g):
    xf = x.astype(jnp.float32)
    y = xf * lax.rsqrt(jnp.mean(xf * xf, axis=-1, keepdims=True) + EPS)
    return (y * g.astype(jnp.float32)).astype(x.dtype)


def swiglu(h, w_gate, w_up, w_down):
    return (jax.nn.silu(h @ w_gate) * (h @ w_up)) @ w_down


def rope_tables(pos, dim, dtype):
    inv = ROPE_THETA ** (-jnp.arange(0, dim, 2, dtype=jnp.float32) / dim)
    ang = pos.astype(jnp.float32)[..., None] * inv
    return jnp.cos(ang)[:, :, None, :].astype(dtype), jnp.sin(ang)[:, :, None, :].astype(dtype)


def apply_rope(t, cos, sin):
    t1, t2 = jnp.split(t, 2, axis=-1)
    return jnp.concatenate([t1 * cos - t2 * sin, t2 * cos + t1 * sin], axis=-1)


def alibi_slopes(n):
    return jnp.asarray(2.0 ** (-8.0 * np.arange(1, n + 1) / n), dtype=jnp.float32)


def causal_depthwise_conv(x, w, b):
    k = w.shape[0]
    y = lax.conv_general_dilated(x, w[:, None, :].astype(x.dtype), window_strides=(1,),
                                 padding=[(k - 1, 0)], dimension_numbers=("NWC", "WIO", "NWC"),
                                 feature_group_count=x.shape[-1])
    return y + b


def segsum_exp(a):
    cs = jnp.cumsum(a, axis=-1)
    diff = cs[..., :, None] - cs[..., None, :]
    L = a.shape[-1]
    mask = jnp.tril(jnp.ones((L, L), dtype=bool))
    return jnp.exp(jnp.where(mask, diff, -jnp.inf))


def ssd_chunked(x, dt, A, Bm, Cm):
    b_, S, H, P = x.shape
    G, N = Bm.shape[-2:]
    E = H // G
    L = SSM_CHUNK
    nc = S // L
    xd = (x.astype(jnp.float32) * dt[..., None]).reshape(b_, nc, L, G, E, P)
    a = jnp.moveaxis((dt * A).reshape(b_, nc, L, G, E), 2, -1)
    a_cs = jnp.cumsum(a, axis=-1)
    Bc = Bm.astype(jnp.float32).reshape(b_, nc, L, G, N)
    Cc = Cm.astype(jnp.float32).reshape(b_, nc, L, G, N)
    CB = jnp.einsum("bclgn,bcsgn->bcgls", Cc, Bc)
    Lmat = segsum_exp(a)
    y_diag = jnp.einsum("bcgls,bcgels,bcsgep->bclgep", CB, Lmat, xd)
    decay_states = jnp.exp(a_cs[..., -1:] - a_cs)
    states = jnp.einsum("bclgn,bcgel,bclgep->bcgepn", Bc, decay_states, xd)
    chunk_decay = jnp.exp(a_cs[..., -1])

    def step(h, inp):
        st, dec = inp
        return h * dec[..., None, None] + st, h

    init = jnp.zeros((b_, G, E, P, N), jnp.float32)
    _, prev = lax.scan(step, init, (jnp.moveaxis(states, 1, 0), jnp.moveaxis(chunk_decay, 1, 0)))
    prev = jnp.moveaxis(prev, 0, 1)
    y_off = jnp.einsum("bclgn,bcgepn,bcgel->bclgep", Cc, prev, jnp.exp(a_cs))
    return (y_diag + y_off).reshape(b_, S, H, P)


def causal_block_attention(q, k, v, scale):
    b_, S, H, dk = q.shape
    nb = S // Q_BLOCK
    qb = jnp.swapaxes(q.reshape(b_, nb, Q_BLOCK, H, dk), 0, 1)
    kpos = jnp.arange(S)

    def one(args):
        qi, i = args
        s = jnp.einsum("bqhd,bkhd->bhqk", qi, k).astype(jnp.float32) * scale
        qpos = i * Q_BLOCK + jnp.arange(Q_BLOCK)
        s = jnp.where(kpos[None, :] <= qpos[:, None], s, -jnp.inf)
        p = jax.nn.softmax(s, axis=-1).astype(v.dtype)
        return jnp.einsum("bhqk,bkhd->bqhd", p, v)

    o = lax.map(one, (qb, jnp.arange(nb)))
    return jnp.swapaxes(o, 0, 1).reshape(b_, S, H, v.shape[-1])


def swa_sink_attention(q, k, v, sinks, pos):
    b_, S, Hq, d = q.shape
    Hkv = k.shape[2]
    G = Hq // Hkv
    W = SWA_WINDOW
    nb = S // W
    qb = q.reshape(b_, nb, W, Hkv, G, d)

    def with_prev(t):
        tb = t.reshape((b_, nb, W) + t.shape[2:])
        prev = jnp.concatenate([jnp.zeros_like(tb[:, :1]), tb[:, :-1]], axis=1)
        return jnp.concatenate([prev, tb], axis=2)

    kb, vb, pk = with_prev(k), with_prev(v), with_prev(pos)
    pq = pos.reshape(b_, nb, W)
    s = jnp.einsum("bnqhgd,bnkhd->bnhgqk", qb, kb).astype(jnp.float32) * (d ** -0.5)
    dist = jnp.abs(pq[:, :, :, None] - pk[:, :, None, :]).astype(jnp.float32)
    slopes = alibi_slopes(Hq).reshape(Hkv, G)
    s = s - slopes[None, None, :, :, None, None] * dist[:, :, None, None]
    qi = jnp.arange(nb)[:, None] * W + jnp.arange(W)[None, :]
    ki = jnp.arange(nb)[:, None] * W - W + jnp.arange(2 * W)[None, :]
    valid = (ki[:, None, :] >= 0) & (ki[:, None, :] <= qi[:, :, None]) & (qi[:, :, None] - ki[:, None, :] < W)
    s = jnp.where(valid[None, :, None, None], s, -jnp.inf)
    sink = jnp.broadcast_to(sinks.astype(jnp.float32).reshape(Hkv, G)[None, None, :, :, None, None],
                            s.shape[:-1] + (1,))
    p = jax.nn.softmax(jnp.concatenate([s, sink], axis=-1), axis=-1)[..., :-1].astype(v.dtype)
    o = jnp.einsum("bnhgqk,bnkhd->bnqhgd", p, vb)
    return o.reshape(b_, S, Hq * d)


def gla_chunked(q, k, v, g):
    b_, S, H, dk = q.shape
    dv = v.shape[-1]
    L = GLA_CHUNK
    nc = S // L
    qc = q.astype(jnp.float32).reshape(b_, nc, L, H, dk)
    kc = k.astype(jnp.float32).reshape(b_, nc, L, H, dk)
    vc = v.astype(jnp.float32).reshape(b_, nc, L, H, dv)
    bcum = jnp.cumsum(g.astype(jnp.float32).reshape(b_, nc, L, H, dk), axis=2)
    b_last = bcum[:, :, -1]
    q_dec = qc * jnp.exp(bcum)
    k_inv = kc * jnp.exp(-bcum)
    k_end = kc * jnp.exp(b_last[:, :, None] - bcum)
    att = jnp.einsum("bclhd,bcshd->bchls", q_dec, k_inv)
    att = jnp.where(jnp.tril(jnp.ones((L, L), dtype=bool)), att, 0.0)
    o_intra = jnp.einsum("bchls,bcshv->bclhv", att, vc)
    kv_chunk = jnp.einsum("bcshd,bcshv->bchdv", k_end, vc)

    def step(state, inp):
        kvc, dec = inp
        return state * dec[..., None] + kvc, state

    init = jnp.zeros((b_, H, dk, dv), jnp.float32)
    _, prev = lax.scan(step, init, (jnp.moveaxis(kv_chunk, 1, 0), jnp.moveaxis(jnp.exp(b_last), 1, 0)))
    prev = jnp.moveaxis(prev, 0, 1)
    o_inter = jnp.einsum("bclhd,bchdv->bclhv", q_dec, prev)
    return (o_intra + o_inter).reshape(b_, S, H, dv).astype(q.dtype)


def ssd_mla_mixer(h, pos, w_in, conv_w, conv_b, dt_bias, a_log, d_skip, ssm_norm,
                  q_a_norm, w_q_b, kv_a_norm, w_kv_b, q_norm, k_norm, w_out):
    b_, S, _ = h.shape
    z, xbc, dt, q_a, kv_a = split_last(h @ w_in, [SSM_D, SSM_XBC, SSM_HEADS, MLA_Q_RANK, MLA_KV_RANK + MLA_ROPE])
    xbc = jax.nn.silu(causal_depthwise_conv(xbc, conv_w, conv_b))
    xs, Bm, Cm = split_last(xbc, [SSM_D, SSM_GROUPS * SSM_STATE, SSM_GROUPS * SSM_STATE])
    xs = xs.reshape(b_, S, SSM_HEADS, SSM_HEAD_DIM)
    dt = jax.nn.softplus((dt + dt_bias).astype(jnp.float32))
    A = -jnp.exp(a_log.astype(jnp.float32))
    y = ssd_chunked(xs, dt, A, Bm.reshape(b_, S, SSM_GROUPS, SSM_STATE), Cm.reshape(b_, S, SSM_GROUPS, SSM_STATE))
    y = (y + d_skip.astype(jnp.float32)[:, None] * xs.astype(jnp.float32)).astype(h.dtype).reshape(b_, S, SSM_D)
    yg = (y * jax.nn.silu(z)).reshape(b_, S, SSM_GROUPS, SSM_D // SSM_GROUPS)
    y = rms_norm(yg, ssm_norm.reshape(SSM_GROUPS, SSM_D // SSM_GROUPS)).reshape(b_, S, SSM_D)
    q = (rms_norm(q_a, q_a_norm) @ w_q_b).reshape(b_, S, MLA_HEADS, MLA_QK)
    kv_c, k_pe = split_last(kv_a, [MLA_KV_RANK, MLA_ROPE])
    kv = (rms_norm(kv_c, kv_a_norm) @ w_kv_b).reshape(b_, S, MLA_HEADS, MLA_NOPE + MLA_V)
    k_nope, v = split_last(kv, [MLA_NOPE, MLA_V])
    k = jnp.concatenate([k_nope, jnp.broadcast_to(k_pe[:, :, None, :], (b_, S, MLA_HEADS, MLA_ROPE))], axis=-1)
    q = rms_norm(q, q_norm)
    k = rms_norm(k, k_norm)
    cos, sin = rope_tables(pos, MLA_ROPE, h.dtype)
    q = jnp.concatenate([q[..., :MLA_NOPE], apply_rope(q[..., MLA_NOPE:], cos, sin)], axis=-1)
    k = jnp.concatenate([k[..., :MLA_NOPE], apply_rope(k[..., MLA_NOPE:], cos, sin)], axis=-1)
    o = causal_block_attention(q, k, v, MLA_QK ** -0.5).reshape(b_, S, MLA_HEADS * MLA_V)
    return jnp.concatenate([y, o], axis=-1) @ w_out


def swa_gla_mixer(h, pos, w_in, q_norm, k_norm, sinks, w_gate_b, gate_bias, gla_norm, w_out):
    b_, S, _ = h.shape
    q, k, v, gq, gk, gv, ga, gr = split_last(h @ w_in, [
        SWA_Q_HEADS * SWA_HD, SWA_KV_HEADS * SWA_HD, SWA_KV_HEADS * SWA_HD,
        GLA_HEADS * GLA_DK, GLA_HEADS * GLA_DK, GLA_HEADS * GLA_DV, GLA_RANK, GLA_HEADS * GLA_DV])
    q = rms_norm(q.reshape(b_, S, SWA_Q_HEADS, SWA_HD), q_norm)
    k = rms_norm(k.reshape(b_, S, SWA_KV_HEADS, SWA_HD), k_norm)
    v = v.reshape(b_, S, SWA_KV_HEADS, SWA_HD)
    o_swa = swa_sink_attention(q, k, v, sinks, pos)
    g = jax.nn.log_sigmoid((ga @ w_gate_b + gate_bias).astype(jnp.float32)) / GLA_TAU
    o = gla_chunked(gq.reshape(b_, S, GLA_HEADS, GLA_DK) * (GLA_DK ** -0.5),
                    gk.reshape(b_, S, GLA_HEADS, GLA_DK),
                    gv.reshape(b_, S, GLA_HEADS, GLA_DV),
                    g.reshape(b_, S, GLA_HEADS, GLA_DK))
    o = rms_norm(o, gla_norm) * jax.nn.silu(gr.reshape(b_, S, GLA_HEADS, GLA_DV))
    return jnp.concatenate([o_swa, o.reshape(b_, S, GLA_HEADS * GLA_DV)], axis=-1) @ w_out


def setup_inputs(seed: int = 0) -> dict:
    key = jax.random.key(seed)
    keys = iter(jax.random.split(key, 64))

    def nrm(shape, fan_in):
        return jax.random.normal(next(keys), shape, jnp.float32) * (fan_in ** -0.5)

    def gain(shape):
        return 1.0 + 0.05 * jax.random.normal(next(keys), shape, jnp.float32)

    def small(shape, s=0.02):
        return s * jax.random.normal(next(keys), shape, jnp.float32)

    x = jax.random.normal(next(keys), (BATCH, SEQ, D_MODEL), jnp.float32)
    positions = jnp.tile(jnp.arange(SEQ, dtype=jnp.int32)[None, :], (BATCH, 1))
    dt0 = jnp.exp(jax.random.uniform(next(keys), (N_EVEN, SSM_HEADS), jnp.float32)
                  * (math.log(0.1) - math.log(0.001)) + math.log(0.001))
    dt_bias = dt0 + jnp.log(-jnp.expm1(-dt0))
    a_log = jnp.log(jax.random.uniform(next(keys), (N_EVEN, SSM_HEADS), jnp.float32, 1.0, 16.0))
    return {
        "x": x,
        "positions": positions,
        "pre_norm": gain((DEPTH, D_MODEL)),
        "pre_w_gate": nrm((DEPTH, D_MODEL, D_FF), D_MODEL),
        "pre_w_up": nrm((DEPTH, D_MODEL, D_FF), D_MODEL),
        "pre_w_down": nrm((DEPTH, D_FF, D_MODEL), D_FF),
        "mix_norm": gain((DEPTH, D_MODEL)),
        "post_norm": gain((DEPTH, D_MODEL)),
        "post_w_gate": nrm((DEPTH, D_MODEL, D_FF), D_MODEL),
        "post_w_up": nrm((DEPTH, D_MODEL, D_FF), D_MODEL),
        "post_w_down": nrm((DEPTH, D_FF, D_MODEL), D_FF),
        "e_w_in": nrm((N_EVEN, D_MODEL, IN_EVEN), D_MODEL),
        "e_conv_w": nrm((N_EVEN, SSM_CONV, SSM_XBC), SSM_CONV),
        "e_conv_b": small((N_EVEN, SSM_XBC)),
        "e_dt_bias": dt_bias,
        "e_a_log": a_log,
        "e_d_skip": gain((N_EVEN, SSM_HEADS)),
        "e_ssm_norm": gain((N_EVEN, SSM_D)),
        "e_q_a_norm": gain((N_EVEN, MLA_Q_RANK)),
        "e_w_q_b": nrm((N_EVEN, MLA_Q_RANK, MLA_HEADS * MLA_QK), MLA_Q_RANK),
        "e_kv_a_norm": gain((N_EVEN, MLA_KV_RANK)),
        "e_w_kv_b": nrm((N_EVEN, MLA_KV_RANK, MLA_HEADS * (MLA_NOPE + MLA_V)), MLA_KV_RANK),
        "e_q_norm": gain((N_EVEN, MLA_QK)),
        "e_k_norm": gain((N_EVEN, MLA_QK)),
        "e_w_out": nrm((N_EVEN, OUT_EVEN, D_MODEL), OUT_EVEN),
        "o_w_in": nrm((N_ODD, D_MODEL, IN_ODD), D_MODEL),
        "o_q_norm": gain((N_ODD, SWA_HD)),
        "o_k_norm": gain((N_ODD, SWA_HD)),
        "o_sinks": small((N_ODD, SWA_Q_HEADS), 0.5),
        "o_w_gate_b": nrm((N_ODD, GLA_RANK, GLA_HEADS * GLA_DK), GLA_RANK),
        "o_gate_bias": small((N_ODD, GLA_HEADS * GLA_DK), 0.1),
        "o_gla_norm": gain((N_ODD, GLA_DV)),
        "o_w_out": nrm((N_ODD, OUT_ODD, D_MODEL), OUT_ODD),
    }


def reference(x, positions, pre_norm, pre_w_gate, pre_w_up, pre_w_down, mix_norm,
              post_norm, post_w_gate, post_w_up, post_w_down,
              e_w_in, e_conv_w, e_conv_b, e_dt_bias, e_a_log, e_d_skip, e_ssm_norm,
              e_q_a_norm, e_w_q_b, e_kv_a_norm, e_w_kv_b, e_q_norm, e_k_norm, e_w_out,
              o_w_in, o_q_norm, o_k_norm, o_sinks, o_w_gate_b, o_gate_bias, o_gla_norm, o_w_out):
    for layer in range(DEPTH):
        x = x + FFN_RES * swiglu(rms_norm(x, pre_norm[layer]), pre_w_gate[layer], pre_w_up[layer], pre_w_down[layer])
        h = rms_norm(x, mix_norm[layer])
        j = layer // 2
        if layer % 2 == 0:
            x = x + ssd_mla_mixer(h, positions, e_w_in[j], e_conv_w[j], e_conv_b[j], e_dt_bias[j], e_a_log[j],
                                  e_d_skip[j], e_ssm_norm[j], e_q_a_norm[j], e_w_q_b[j], e_kv_a_norm[j],
                                  e_w_kv_b[j], e_q_norm[j], e_k_norm[j], e_w_out[j])
        else:
            x = x + swa_gla_mixer(h, positions, o_w_in[j], o_q_norm[j], o_k_norm[j], o_sinks[j],
                                  o_w_gate_b[j], o_gate_bias[j], o_gla_norm[j], o_w_out[j])
        x = x + FFN_RES * swiglu(rms_norm(x, post_norm[layer]), post_w_gate[layer], post_w_up[layer], post_w_down[layer])
    return x
```

```python
import functools
import math

import jax
import jax.numpy as jnp
import numpy as np
from jax import lax
from jax.experimental import pallas as pl
from jax.experimental.pallas import tpu as pltpu

F32 = jnp.float32
BF16 = jnp.bfloat16
HIGHEST = lax.Precision.HIGHEST

D_MODEL = 1024
D_FF = 2816
FFN_RES = 0.5
EPS = 1e-6
SSM_HEADS = 16
SSM_HEAD_DIM = 64
SSM_D = SSM_HEADS * SSM_HEAD_DIM
SSM_GROUPS = 4
SSM_STATE = 128
SSM_CONV = 4
SSM_CHUNK = 128
SSM_XBC = SSM_D + 2 * SSM_GROUPS * SSM_STATE
MLA_HEADS = 8
MLA_Q_RANK = 384
MLA_KV_RANK = 256
MLA_NOPE = 64
MLA_ROPE = 32
MLA_QK = MLA_NOPE + MLA_ROPE
MLA_V = 64
ROPE_THETA = 10000.0
SWA_Q_HEADS = 8
SWA_KV_HEADS = 2
SWA_HD = 64
SWA_WINDOW = 128
GLA_HEADS = 4
GLA_DK = 64
GLA_DV = 128
GLA_RANK = 16
GLA_TAU = 16.0
GLA_CHUNK = 64

LANES = 128
VMEM_LIMIT_BYTES = 56 * 1024 * 1024

FF_CHUNK = 256
TOKEN_TILE = 512
MLA_BLOCK = 512
SWA_TILE = 512
GLA_TILE = 256


def _params(*sem):
    return pltpu.CompilerParams(dimension_semantics=sem, vmem_limit_bytes=VMEM_LIMIT_BYTES)


def _resident(shape):
    nd = len(shape)
    return pl.BlockSpec(shape, lambda *_: (0,) * nd)


def _rms(x, gain):
    return x * lax.rsqrt(jnp.mean(x * x, axis=-1, keepdims=True) + EPS) * gain


def _silu(x):
    return x * jax.nn.sigmoid(x)


def _dot(a, b, **kw):
    return jnp.dot(a, b, preferred_element_type=F32, **kw)


def _dot_nt(a, b):
    return lax.dot_general(a, b, (((1,), (1,)), ((), ())), preferred_element_type=F32)


def _dot_tn(a, b):
    return lax.dot_general(a, b, (((0,), (0,)), ((), ())), preferred_element_type=F32)


def _ffn_body(x_ref, g_ref, wgu_ref, wd_ref, o_ref, h_ref, acc_ref, *, n_chunks):
    h_ref[...] = _rms(x_ref[...], g_ref[...]).astype(BF16)
    acc_ref[...] = jnp.zeros_like(acc_ref)

    def step(c, carry):
        gu = _dot(h_ref[...], wgu_ref[c])
        act = (_silu(gu[:, :FF_CHUNK]) * gu[:, FF_CHUNK:]).astype(BF16)
        acc_ref[...] += _dot(act, wd_ref[c])
        return carry

    lax.fori_loop(0, n_chunks, step, 0)
    o_ref[...] = x_ref[...] + FFN_RES * acc_ref[...]


def _ffn(x2, norm, w_gate, w_up, w_down):
    t = x2.shape[0]
    tm = min(TOKEN_TILE, t)
    n_chunks = D_FF // FF_CHUNK
    wg = w_gate.reshape(D_MODEL, n_chunks, FF_CHUNK).transpose(1, 0, 2)
    wu = w_up.reshape(D_MODEL, n_chunks, FF_CHUNK).transpose(1, 0, 2)
    wgu = jnp.concatenate([wg, wu], axis=-1).astype(BF16)
    wd = w_down.reshape(n_chunks, FF_CHUNK, D_MODEL).astype(BF16)
    return pl.pallas_call(
        functools.partial(_ffn_body, n_chunks=n_chunks),
        out_shape=jax.ShapeDtypeStruct((t, D_MODEL), F32),
        grid=(t // tm,),
        in_specs=[
            pl.BlockSpec((tm, D_MODEL), lambda i: (i, 0)),
            _resident((1, D_MODEL)),
            _resident(wgu.shape),
            _resident(wd.shape),
        ],
        out_specs=pl.BlockSpec((tm, D_MODEL), lambda i: (i, 0)),
        scratch_shapes=[pltpu.VMEM((tm, D_MODEL), BF16), pltpu.VMEM((tm, D_MODEL), F32)],
        compiler_params=_params("parallel"),
        name="swiglu_half_step",
    )(x2, norm.reshape(1, D_MODEL), wgu, wd)


def _head_norm(t, gain, n_valid):
    ms = jnp.sum(t * t, axis=-1, keepdims=True) * (1.0 / n_valid)
    return t * lax.rsqrt(ms + EPS) * gain


def _inproj_body(x_ref, g_ref, *refs, head_norms):
    n_out = len(head_norms)
    w_refs = refs[:n_out]
    n_gain = sum(1 for hn in head_norms if hn)
    gain_refs = refs[n_out:n_out + n_gain]
    o_refs = refs[n_out + n_gain:]
    h = _rms(x_ref[...], g_ref[...]).astype(BF16)
    gi = 0
    for w_ref, o_ref, hn in zip(w_refs, o_refs, head_norms):
        y = _dot(h, w_ref[...])
        if hn:
            gain = gain_refs[gi][...]
            gi += 1
            for hd in range(y.shape[1] // LANES):
                sl = slice(hd * LANES, (hd + 1) * LANES)
                o_ref[:, sl] = _head_norm(y[:, sl], gain, hn).astype(o_ref.dtype)
        else:
            o_ref[...] = y.astype(o_ref.dtype)


def _inproj(x2, norm, weights, out_dtypes, head_norms, gains):
    t = x2.shape[0]
    tm = min(TOKEN_TILE, t)
    in_specs = [pl.BlockSpec((tm, D_MODEL), lambda i: (i, 0)), _resident((1, D_MODEL))]
    in_specs += [_resident(w.shape) for w in weights]
    in_specs += [_resident(g.shape) for g in gains]
    return pl.pallas_call(
        functools.partial(_inproj_body, head_norms=tuple(head_norms)),
        out_shape=[jax.ShapeDtypeStruct((t, w.shape[1]), dt) for w, dt in zip(weights, out_dtypes)],
        grid=(t // tm,),
        in_specs=in_specs,
        out_specs=[pl.BlockSpec((tm, w.shape[1]), lambda i: (i, 0)) for w in weights],
        compiler_params=_params("parallel"),
        name="mixer_in_proj",
    )(x2, norm.reshape(1, D_MODEL), *weights, *gains)


def _outproj_body(x_ref, *refs):
    n_in = (len(refs) - 1) // 2
    acc = x_ref[...]
    for a_ref, w_ref in zip(refs[:n_in], refs[n_in:2 * n_in]):
        acc = acc + _dot(a_ref[...], w_ref[...])
    refs[-1][...] = acc


def _outproj(x2, acts, weights):
    t = x2.shape[0]
    tm = min(TOKEN_TILE, t)
    in_specs = [pl.BlockSpec((tm, D_MODEL), lambda i: (i, 0))]
    in_specs += [pl.BlockSpec((tm, a.shape[1]), lambda i: (i, 0)) for a in acts]
    in_specs += [_resident(w.shape) for w in weights]
    return pl.pallas_call(
        _outproj_body,
        out_shape=jax.ShapeDtypeStruct((t, D_MODEL), F32),
        grid=(t // tm,),
        in_specs=in_specs,
        out_specs=pl.BlockSpec((tm, D_MODEL), lambda i: (i, 0)),
        compiler_params=_params("parallel"),
        name="mixer_out_proj",
    )(x2, *acts, *weights)


def _rope_body(pos_ref, inv_ref, cos_ref, sin_ref):
    ang = pos_ref[...] * inv_ref[...]
    cos_ref[...] = jnp.cos(ang)
    sin_ref[...] = jnp.sin(ang)


def _rope_tables(positions):
    b, s = positions.shape
    half = MLA_ROPE // 2
    inv = ROPE_THETA ** (-jnp.arange(0, MLA_ROPE, 2, dtype=F32) / MLA_ROPE)
    per_row = LANES // half
    rows = b * s // per_row
    pos_rep = jnp.repeat(positions.astype(F32).reshape(rows, per_row), half, axis=1)
    inv_row = jnp.tile(inv, per_row).reshape(1, LANES)
    tr = min(1024, rows)
    cos_c, sin_c = pl.pallas_call(
        _rope_body,
        out_shape=[jax.ShapeDtypeStruct((rows, LANES), F32)] * 2,
        grid=(rows // tr,),
        in_specs=[pl.BlockSpec((tr, LANES), lambda i: (i, 0)), _resident((1, LANES))],
        out_specs=[pl.BlockSpec((tr, LANES), lambda i: (i, 0))] * 2,
        compiler_params=_params("parallel"),
        name="rope_tables",
    )(pos_rep, inv_row)
    cos = cos_c.reshape(b * s, half)
    sin = sin_c.reshape(b * s, half)
    ones = jnp.ones((b * s, MLA_NOPE), F32)
    tail = jnp.ones((b * s, LANES - MLA_QK), F32)
    cos_full = jnp.concatenate([ones, cos, cos, tail], axis=1)
    sin_full = jnp.concatenate([0 * ones, -sin, sin, 0 * tail], axis=1)
    return cos_full, sin_full


def _mla_prep_body(qa_ref, kvc_ref, kpe_ref, cos_ref, sin_ref, qan_ref, kvn_ref, wq_ref, wk_ref,
                   wv_ref, qn_ref, kn_ref, q_ref, k_ref, v_ref):
    cosf = cos_ref[...]
    sinf = sin_ref[...]
    lane = lax.broadcasted_iota(jnp.int32, cosf.shape, 1)
    first_half = lane < MLA_NOPE + MLA_ROPE // 2

    def norm_rope(t, gain):
        t = _head_norm(t, gain, MLA_QK)
        partner = jnp.where(first_half, pltpu.roll(t, LANES - MLA_ROPE // 2, 1),
                            pltpu.roll(t, MLA_ROPE // 2, 1))
        return t * cosf + partner * sinf

    qa = _rms(qa_ref[...].astype(F32), qan_ref[...]).astype(BF16)
    q_all = _dot(qa, wq_ref[...])
    kvc = _rms(kvc_ref[...].astype(F32), kvn_ref[...]).astype(BF16)
    k_all = _dot(kvc, wk_ref[...])
    v_ref[...] = _dot(kvc, wv_ref[...]).astype(v_ref.dtype)
    kpe = kpe_ref[...]
    qgain = qn_ref[...] * (MLA_QK ** -0.5)
    for hd in range(MLA_HEADS):
        sl = slice(hd * LANES, (hd + 1) * LANES)
        q_ref[:, sl] = norm_rope(q_all[:, sl], qgain).astype(q_ref.dtype)
        k_ref[:, sl] = norm_rope(k_all[:, sl] + kpe, kn_ref[...]).astype(k_ref.dtype)


def _mla_prep(qa, kvc, kpe, cos_full, sin_full, qan, kvn, wq, wk, wv, qn, kn):
    t = qa.shape[0]
    tm = min(TOKEN_TILE, t)
    row = lambda n: pl.BlockSpec((tm, n), lambda i: (i, 0))
    hw = MLA_HEADS * LANES
    return pl.pallas_call(
        _mla_prep_body,
        out_shape=[jax.ShapeDtypeStruct((t, hw), BF16), jax.ShapeDtypeStruct((t, hw), BF16),
                   jax.ShapeDtypeStruct((t, MLA_HEADS * MLA_V), BF16)],
        grid=(t // tm,),
        in_specs=[row(MLA_Q_RANK), row(MLA_KV_RANK), row(LANES), row(LANES), row(LANES),
                  _resident(qan.shape), _resident(kvn.shape), _resident(wq.shape),
                  _resident(wk.shape), _resident(wv.shape), _resident(qn.shape), _resident(kn.shape)],
        out_specs=[row(hw), row(hw), row(MLA_HEADS * MLA_V)],
        compiler_params=_params("parallel"),
        name="mla_prep",
    )(qa, kvc, kpe, cos_full, sin_full, qan, kvn, wq, wk, wv, qn, kn)


def _mla_attn_body(q_ref, k_ref, v_ref, o_ref, m_ref, l_ref, acc_ref, *, blk):
    i = pl.program_id(2)
    row = lax.broadcasted_iota(jnp.int32, (blk, blk), 0)
    col = lax.broadcasted_iota(jnp.int32, (blk, blk), 1)
    lane = lax.broadcasted_iota(jnp.int32, (blk, LANES), 1)

    for hd in range(2):
        sl = slice(hd * LANES, (hd + 1) * LANES)
        m_ref[...] = jnp.full_like(m_ref, -jnp.inf)
        l_ref[...] = jnp.zeros_like(l_ref)
        acc_ref[...] = jnp.zeros_like(acc_ref)

        def block(j, masked):
            ks = pl.multiple_of(j * blk, blk)
            s = _dot_nt(q_ref[0, :, sl], k_ref[0, pl.ds(ks, blk), sl])
            if masked:
                s = jnp.where(col <= row, s, -jnp.inf)
            m_old = m_ref[...]
            m_new = jnp.maximum(m_old, jnp.max(s, axis=-1, keepdims=True))
            alpha = jnp.exp(m_old - m_new)
            p = jnp.exp(s - m_new)
            l_ref[...] = alpha * l_ref[...] + jnp.sum(p, axis=-1, keepdims=True)
            acc_ref[...] = alpha * acc_ref[...] + _dot(p.astype(BF16), v_ref[0, pl.ds(ks, blk), :])
            m_ref[...] = m_new

        def off_diag(j, carry):
            block(j, False)
            return carry

        lax.fori_loop(0, i, off_diag, 0)
        block(i, True)
        res = acc_ref[...] / l_ref[...]
        if hd == 0:
            out = res
        else:
            out = jnp.where(lane < MLA_V, out, res)
    o_ref[0] = out.astype(o_ref.dtype)


def _mla_attention(q, k, v):
    b, s, _ = q.shape
    blk = min(MLA_BLOCK, s)
    pairs = MLA_HEADS // 2
    return pl.pallas_call(
        functools.partial(_mla_attn_body, blk=blk),
        out_shape=jax.ShapeDtypeStruct((b, s, MLA_HEADS * MLA_V), BF16),
        grid=(b, pairs, s // blk),
        in_specs=[
            pl.BlockSpec((1, blk, 2 * LANES), lambda bb, p, i: (bb, i, p)),
            pl.BlockSpec((1, s, 2 * LANES), lambda bb, p, i: (bb, 0, p)),
            pl.BlockSpec((1, s, LANES), lambda bb, p, i: (bb, 0, p)),
        ],
        out_specs=pl.BlockSpec((1, blk, LANES), lambda bb, p, i: (bb, i, p)),
        scratch_shapes=[pltpu.VMEM((blk, 1), F32), pltpu.VMEM((blk, 1), F32),
                        pltpu.VMEM((blk, LANES), F32)],
        compiler_params=_params("parallel", "parallel", "arbitrary"),
        name="mla_causal_attention",
    )(q, k, v)


def _softplus(x):
    return jnp.maximum(x, 0.0) + jnp.log(1.0 + jnp.exp(-jnp.abs(x)))


def _ssd_body(xbc_ref, z_ref, dt_ref, cw_ref, cb_ref, dtb_ref, alog_ref, dsk_ref, nrm_ref,
              e64_ref, e128_ref, y_ref, xpad_ref, st_ref):
    L = SSM_CHUNK
    pad = 8

    @pl.when(pl.program_id(1) == 0)
    def _():
        xpad_ref[0:pad, :] = jnp.zeros((pad, SSM_XBC), F32)
        st_ref[...] = jnp.zeros_like(st_ref)

    x_raw = xbc_ref[0].astype(F32)
    xpad_ref[pad:pad + L, :] = x_raw
    conv = cb_ref[...] + cw_ref[SSM_CONV - 1:SSM_CONV, :] * x_raw
    for back in range(1, SSM_CONV):
        tap = SSM_CONV - 1 - back
        conv = conv + cw_ref[tap:tap + 1, :] * xpad_ref[pad - back:pad - back + L, :]
    xpad_ref[0:pad, :] = x_raw[L - pad:, :]
    act = _silu(conv)
    xs = act[:, :SSM_D]
    gn = SSM_GROUPS * SSM_STATE
    b_all = act[:, SSM_D:SSM_D + gn]
    c_all = act[:, SSM_D + gn:]

    lane = lax.broadcasted_iota(jnp.int32, (L, LANES), 1)
    row = lax.broadcasted_iota(jnp.int32, (L, L), 0)
    col = lax.broadcasted_iota(jnp.int32, (L, L), 1)
    causal = col <= row
    dt = jnp.where(lane < SSM_HEADS, _softplus(dt_ref[0] + dtb_ref[...]), 0.0)
    a = dt * (-jnp.exp(alog_ref[...]))
    cs = _dot(causal.astype(F32), a, precision=HIGHEST)
    cs_t = cs.T
    cs_last = cs[L - 1:L, :]
    ecs = jnp.exp(cs)
    stacked = jnp.concatenate([dt, ecs, jnp.exp(cs_last - cs)], axis=0)
    wide = _dot(stacked, e64_ref[...], precision=HIGHEST)
    dt_w, ecs_w, dec_w = wide[:L], wide[L:2 * L], wide[2 * L:]
    cs_col = _dot(cs, e128_ref[...], precision=HIGHEST)

    xd = xs * dt_w
    xd_b = xd.astype(BF16)
    xdw_b = (xd * dec_w).astype(BF16)
    chunk_decay = ecs_w[L - 1:L, :]

    heads_per_group = SSM_HEADS // SSM_GROUPS
    gw = heads_per_group * SSM_HEAD_DIM
    y_parts = []
    yoff_parts = []
    for g in range(SSM_GROUPS):
        bg = b_all[:, g * SSM_STATE:(g + 1) * SSM_STATE]
        cg = c_all[:, g * SSM_STATE:(g + 1) * SSM_STATE].astype(BF16)
        cb = _dot_nt(cg, bg.astype(BF16))
        st_g = st_ref[:, g * gw:(g + 1) * gw]
        yoff_parts.append(_dot(cg, st_g.astype(BF16)))
        for pr in range(heads_per_group // 2):
            ms = []
            for hh in range(2):
                hd = g * heads_per_group + 2 * pr + hh
                diff = cs_col[:, hd * LANES:(hd + 1) * LANES] - cs_t[hd:hd + 1, :]
                ms.append(cb * jnp.exp(jnp.where(causal, diff, -jnp.inf)))
            lhs = jnp.concatenate(ms, axis=1).astype(BF16)
            pi = (g * heads_per_group) // 2 + pr
            pair = xd_b[:, pi * LANES:(pi + 1) * LANES]
            zero = jnp.zeros_like(pair)
            rhs = jnp.concatenate([jnp.where(lane < SSM_HEAD_DIM, pair, zero),
                                   jnp.where(lane < SSM_HEAD_DIM, zero, pair)], axis=0)
            y_parts.append(_dot(lhs, rhs))
        upd = _dot(bg.T.astype(BF16), xdw_b[:, g * gw:(g + 1) * gw])
        st_ref[:, g * gw:(g + 1) * gw] = st_g * chunk_decay[:, g * gw:(g + 1) * gw] + upd

    y = jnp.concatenate(y_parts, axis=1) + ecs_w * jnp.concatenate(yoff_parts, axis=1)
    y = (y + dsk_ref[...] * xs) * _silu(z_ref[0].astype(F32))
    for g in range(SSM_GROUPS):
        sl = slice(g * gw, (g + 1) * gw)
        y_ref[0, :, sl] = _rms(y[:, sl], nrm_ref[:, sl]).astype(y_ref.dtype)


def _ssd(xbc, z, dt, conv_w, conv_b, dt_bias, a_log, d_skip, ssm_norm):
    b, s, _ = xbc.shape
    L = SSM_CHUNK
    heads = np.arange(LANES)[:, None]
    e64 = jnp.asarray(heads == (np.arange(SSM_D)[None, :] // SSM_HEAD_DIM), F32)
    e128 = jnp.asarray(heads == (np.arange(SSM_HEADS * LANES)[None, :] // LANES), F32)
    padh = lambda v: jnp.pad(v.reshape(1, SSM_HEADS), ((0, 0), (0, LANES - SSM_HEADS)))
    dsk = jnp.repeat(d_skip, SSM_HEAD_DIM).reshape(1, SSM_D)
    blk = lambda n: pl.BlockSpec((1, L, n), lambda bb, c: (bb, c, 0))
    return pl.pallas_call(
        _ssd_body,
        out_shape=jax.ShapeDtypeStruct((b, s, SSM_D), BF16),
        grid=(b, s // L),
        in_specs=[blk(SSM_XBC), blk(SSM_D), blk(LANES),
                  _resident((SSM_CONV, SSM_XBC)), _resident((1, SSM_XBC)), _resident((1, LANES)),
                  _resident((1, LANES)), _resident((1, SSM_D)), _resident((1, SSM_D)),
                  _resident(e64.shape), _resident(e128.shape)],
        out_specs=blk(SSM_D),
        scratch_shapes=[pltpu.VMEM((L + 8, SSM_XBC), F32), pltpu.VMEM((SSM_STATE, SSM_D), F32)],
        compiler_params=_params("parallel", "arbitrary"),
        name="ssd_scan",
    )(xbc, z, dt, conv_w, conv_b.reshape(1, SSM_XBC), padh(dt_bias), padh(a_log), dsk,
      ssm_norm.reshape(1, SSM_D), e64, e128)


def _swa_body(q_ref, k_ref, v_ref, pc_ref, pr_ref, sink_ref, o_ref, *, tq):
    W = SWA_WINDOW
    grp = SWA_Q_HEADS // SWA_KV_HEADS
    slopes = [2.0 ** (-8.0 * (h + 1) / SWA_Q_HEADS) for h in range(SWA_Q_HEADS)]
    i = pl.program_id(1)
    lane = lax.broadcasted_iota(jnp.int32, (W, LANES), 1)
    qi = lax.broadcasted_iota(jnp.int32, (W, 2 * W), 0)
    ki = lax.broadcasted_iota(jnp.int32, (W, 2 * W), 1)
    for jb in range(tq // W):
        j = i * (tq // W) + jb
        kb0 = jnp.maximum(j - 1, 0)
        ks = pl.multiple_of(kb0 * W, W)
        rel = (j - kb0) * W + qi - ki
        valid = (rel >= 0) & (rel < W)
        pq = pc_ref[0, jb * W:(jb + 1) * W, :]
        pk = jnp.concatenate([pr_ref[0, kb0], pr_ref[0, kb0 + 1]], axis=1)
        dist = jnp.abs(pq - pk).astype(F32)
        rows = slice(jb * W, (jb + 1) * W)
        for kv in range(SWA_KV_HEADS):
            kk = k_ref[0, pl.ds(ks, 2 * W), kv * LANES:(kv + 1) * LANES]
            vv = v_ref[0, pl.ds(ks, 2 * W), kv * LANES:(kv + 1) * LANES]
            qs = jnp.concatenate(
                [q_ref[0, rows, (kv * grp + g) * LANES:(kv * grp + g + 1) * LANES] for g in range(grp)],
                axis=0)
            s_all = _dot_nt(qs, kk)
            outs = []
            for g in range(grp):
                hd = kv * grp + g
                s = s_all[g * W:(g + 1) * W] - slopes[hd] * dist
                s = jnp.where(valid, s, -jnp.inf)
                sink = sink_ref[0:1, hd:hd + 1]
                m = jnp.maximum(jnp.max(s, axis=-1, keepdims=True), sink)
                p = jnp.exp(s - m)
                den = jnp.sum(p, axis=-1, keepdims=True) + jnp.exp(sink - m)
                outs.append(_dot(p.astype(BF16), vv) / den)
            for pr in range(grp // 2):
                o_pair = jnp.where(lane < SWA_HD, outs[2 * pr], outs[2 * pr + 1])
                c0 = (kv * grp // 2 + pr) * LANES
                o_ref[0, rows, c0:c0 + LANES] = o_pair.astype(o_ref.dtype)


def _swa(q, k, v, pos, sinks):
    b, s, _ = q.shape
    tq = min(SWA_TILE, s)
    W = SWA_WINDOW
    nb = s // W
    pos_col = pos.reshape(b, s, 1)
    pos_row = jnp.concatenate([pos.reshape(b, nb, 1, W), jnp.zeros((b, 1, 1, W), pos.dtype)], axis=1)
    return pl.pallas_call(
        functools.partial(_swa_body, tq=tq),
        out_shape=jax.ShapeDtypeStruct((b, s, SWA_Q_HEADS * SWA_HD), BF16),
        grid=(b, s // tq),
        in_specs=[
            pl.BlockSpec((1, tq, SWA_Q_HEADS * LANES), lambda bb, i: (bb, i, 0)),
            pl.BlockSpec((1, s, SWA_KV_HEADS * LANES), lambda bb, i: (bb, 0, 0)),
            pl.BlockSpec((1, s, SWA_KV_HEADS * LANES), lambda bb, i: (bb, 0, 0)),
            pl.BlockSpec((1, tq, 1), lambda bb, i: (bb, i, 0)),
            pl.BlockSpec((1, nb + 1, 1, W), lambda bb, i: (bb, 0, 0, 0)),
            _resident((1, SWA_Q_HEADS)),
        ],
        out_specs=pl.BlockSpec((1, tq, SWA_Q_HEADS * SWA_HD), lambda bb, i: (bb, i, 0)),
        compiler_params=_params("parallel", "parallel"),
        name="swa_sink_attention",
    )(q, k, v, pos_col, pos_row, sinks.reshape(1, SWA_Q_HEADS))


def _log_sigmoid(x):
    return jnp.minimum(x, 0.0) - jnp.log(1.0 + jnp.exp(-jnp.abs(x)))


def _gla_body(q_ref, k_ref, v_ref, ga_ref, gr_ref, wgb_ref, gb_ref, nrm_ref, o_ref, st_ref, *, tg):
    C = GLA_CHUNK

    @pl.when(pl.program_id(1) == 0)
    def _():
        st_ref[...] = jnp.zeros_like(st_ref)

    row = lax.broadcasted_iota(jnp.int32, (tg, tg), 0)
    col = lax.broadcasted_iota(jnp.int32, (tg, tg), 1)
    shift = C.bit_length() - 1
    same_chunk = lax.shift_right_logical(row, shift) == lax.shift_right_logical(col, shift)
    intra = same_chunk & (col <= row)
    g = _log_sigmoid(_dot(ga_ref[0], wgb_ref[...]) + gb_ref[...]) * (1.0 / GLA_TAU)
    bcum = _dot(intra.astype(F32), g, precision=HIGHEST)
    btot = _dot(same_chunk.astype(F32), g, precision=HIGHEST)
    q_dec = (q_ref[0].astype(F32) * jnp.exp(bcum)).astype(BF16)
    kf = k_ref[0].astype(F32)
    k_inv = (kf * jnp.exp(-bcum)).astype(BF16)
    k_end = (kf * jnp.exp(btot - bcum)).astype(BF16)
    chunk_dec = jnp.exp(btot)
    for hd in range(GLA_HEADS):
        sl = slice(hd * LANES, (hd + 1) * LANES)
        vh = v_ref[0, :, sl]
        att = jnp.where(intra, _dot_nt(q_dec[:, sl], k_inv[:, sl]), 0.0)
        o = _dot(att.astype(BF16), vh)
        inter = []
        for c in range(tg // C):
            rows = slice(c * C, (c + 1) * C)
            st = st_ref[hd]
            inter.append(_dot_nt(q_dec[rows, sl], st.astype(BF16)))
            st_ref[hd] = st * chunk_dec[c * C:c * C + 1, sl] + _dot_tn(vh[rows], k_end[rows, sl])
        o = o + jnp.concatenate(inter, axis=0)
        o = _rms(o, nrm_ref[...]) * _silu(gr_ref[0, :, sl].astype(F32))
        o_ref[0, :, sl] = o.astype(o_ref.dtype)


def _gla(q, k, v, ga, gr, wgb, gate_bias, gla_norm):
    b, s, _ = q.shape
    tg = min(GLA_TILE, s)
    hw = GLA_HEADS * LANES
    blk = lambda n: pl.BlockSpec((1, tg, n), lambda bb, i: (bb, i, 0))
    return pl.pallas_call(
        functools.partial(_gla_body, tg=tg),
        out_shape=jax.ShapeDtypeStruct((b, s, GLA_HEADS * GLA_DV), BF16),
        grid=(b, s // tg),
        in_specs=[blk(hw), blk(hw), blk(GLA_HEADS * GLA_DV), blk(LANES), blk(GLA_HEADS * GLA_DV),
                  _resident(wgb.shape), _resident((1, hw)), _resident((1, GLA_DV))],
        out_specs=blk(GLA_HEADS * GLA_DV),
        scratch_shapes=[pltpu.VMEM((GLA_HEADS, GLA_DV, LANES), F32)],
        compiler_params=_params("parallel", "arbitrary"),
        name="gla_chunked",
    )(q, k, v, ga, gr, wgb, gate_bias, gla_norm.reshape(1, GLA_DV))


def _pad_heads(w, n_heads, width, lanes=LANES, offset=0):
    lead = w.shape[:-1]
    w = w.reshape(lead + (n_heads, width))
    cfg = [(0, 0)] * len(lead) + [(0, 0), (offset, lanes - width - offset)]
    return jnp.pad(w, cfg).reshape(lead + (n_heads * lanes,))


def _pad_cols(w, total, offset=0):
    return jnp.pad(w, [(0, 0)] * (w.ndim - 1) + [(offset, total - w.shape[-1] - offset)])


def _even_mixer(x2, b, s, pos, mix_norm, w_in, conv_w, conv_b, dt_bias, a_log, d_skip, ssm_norm,
                q_a_norm, w_q_b, kv_a_norm, w_kv_b, q_norm, k_norm, w_out):
    t = b * s
    o0 = 0
    cols = {}
    for name, n in (("z", SSM_D), ("xbc", SSM_XBC), ("dt", SSM_HEADS), ("qa", MLA_Q_RANK),
                    ("kvc", MLA_KV_RANK), ("kpe", MLA_ROPE)):
        cols[name] = w_in[:, o0:o0 + n]
        o0 += n
    weights = [cols["z"], cols["xbc"], _pad_cols(cols["dt"], LANES), cols["qa"], cols["kvc"],
               _pad_cols(cols["kpe"], LANES, offset=MLA_NOPE)]
    weights = [w.astype(BF16) for w in weights]
    z, xbc, dt, qa, kvc, kpe = _inproj(x2, mix_norm, weights, [BF16, BF16, F32, BF16, BF16, F32],
                                       [0] * 6, [])
    y = _ssd(xbc.reshape(b, s, -1), z.reshape(b, s, -1), dt.reshape(b, s, -1), conv_w, conv_b,
             dt_bias, a_log, d_skip, ssm_norm).reshape(t, SSM_D)

    cos_full, sin_full = _rope_tables(pos)
    wq = _pad_heads(w_q_b, MLA_HEADS, MLA_QK).astype(BF16)
    w_kv = w_kv_b.reshape(MLA_KV_RANK, MLA_HEADS, MLA_NOPE + MLA_V)
    wk = _pad_heads(w_kv[:, :, :MLA_NOPE].reshape(MLA_KV_RANK, -1), MLA_HEADS, MLA_NOPE).astype(BF16)
    wv = w_kv[:, :, MLA_NOPE:].reshape(MLA_KV_RANK, MLA_HEADS * MLA_V).astype(BF16)
    q, k, v = _mla_prep(qa, kvc, kpe, cos_full, sin_full, q_a_norm.reshape(1, -1),
                        kv_a_norm.reshape(1, -1), wq, wk, wv,
                        _pad_cols(q_norm.reshape(1, -1), LANES), _pad_cols(k_norm.reshape(1, -1), LANES))
    o = _mla_attention(q.reshape(b, s, -1), k.reshape(b, s, -1), v.reshape(b, s, -1)).reshape(t, -1)
    return _outproj(x2, [y, o], [w_out[:SSM_D].astype(BF16), w_out[SSM_D:].astype(BF16)])


def _odd_mixer(x2, b, s, pos, mix_norm, w_in, q_norm, k_norm, sinks, w_gate_b, gate_bias, gla_norm,
               w_out):
    t = b * s
    sizes = [SWA_Q_HEADS * SWA_HD, SWA_KV_HEADS * SWA_HD, SWA_KV_HEADS * SWA_HD,
             GLA_HEADS * GLA_DK, GLA_HEADS * GLA_DK, GLA_HEADS * GLA_DV, GLA_RANK, GLA_HEADS * GLA_DV]
    parts, o0 = [], 0
    for n in sizes:
        parts.append(w_in[:, o0:o0 + n])
        o0 += n
    wq, wk, wv, wgq, wgk, wgv, wga, wgr = parts
    v_dup = jnp.repeat(wv.reshape(D_MODEL, SWA_KV_HEADS, 1, SWA_HD), 2, axis=2).reshape(D_MODEL, -1)
    weights = [_pad_heads(wq, SWA_Q_HEADS, SWA_HD), _pad_heads(wk, SWA_KV_HEADS, SWA_HD), v_dup,
               _pad_heads(wgq * (GLA_DK ** -0.5), GLA_HEADS, GLA_DK), _pad_heads(wgk, GLA_HEADS, GLA_DK),
               wgv, _pad_cols(wga, LANES), wgr]
    weights = [w.astype(BF16) for w in weights]
    gains = [_pad_cols(q_norm.reshape(1, -1) * (SWA_HD ** -0.5), LANES),
             _pad_cols(k_norm.reshape(1, -1), LANES)]
    q, k, v, gq, gk, gv, ga, gr = _inproj(x2, mix_norm, weights, [BF16] * 8,
                                          [SWA_HD, SWA_HD, 0, 0, 0, 0, 0, 0], gains)
    o_swa = _swa(q.reshape(b, s, -1), k.reshape(b, s, -1), v.reshape(b, s, -1), pos, sinks)
    wgb = _pad_heads(jnp.pad(w_gate_b, ((0, LANES - GLA_RANK), (0, 0))), GLA_HEADS, GLA_DK).astype(BF16)
    gb = _pad_heads(gate_bias.reshape(1, -1), GLA_HEADS, GLA_DK)
    o_gla = _gla(gq.reshape(b, s, -1), gk.reshape(b, s, -1), gv.reshape(b, s, -1), ga.reshape(b, s, -1),
                 gr.reshape(b, s, -1), wgb, gb, gla_norm)
    n_swa = SWA_Q_HEADS * SWA_HD
    return _outproj(x2, [o_swa.reshape(t, -1), o_gla.reshape(t, -1)],
                    [w_out[:n_swa].astype(BF16), w_out[n_swa:].astype(BF16)])


def kernel(x, positions, pre_norm, pre_w_gate, pre_w_up, pre_w_down, mix_norm, post_norm, post_w_gate,
           post_w_up, post_w_down, e_w_in, e_conv_w, e_conv_b, e_dt_bias, e_a_log, e_d_skip, e_ssm_norm,
           e_q_a_norm, e_w_q_b, e_kv_a_norm, e_w_kv_b, e_q_norm, e_k_norm, e_w_out, o_w_in, o_q_norm,
           o_k_norm, o_sinks, o_w_gate_b, o_gate_bias, o_gla_norm, o_w_out):
    b, s, d = x.shape
    depth = pre_norm.shape[0]
    x2 = x.reshape(b * s, d)
    for layer in range(depth):
        x2 = _ffn(x2, pre_norm[layer], pre_w_gate[layer], pre_w_up[layer], pre_w_down[layer])
        j = layer // 2
        if layer % 2 == 0:
            x2 = _even_mixer(x2, b, s, positions, mix_norm[layer], e_w_in[j], e_conv_w[j], e_conv_b[j],
                             e_dt_bias[j], e_a_log[j], e_d_skip[j], e_ssm_norm[j], e_q_a_norm[j],
                             e_w_q_b[j], e_kv_a_norm[j], e_w_kv_b[j], e_q_norm[j], e_k_norm[j], e_w_out[j])
        else:
            x2 = _odd_mixer(x2, b, s, positions, mix_norm[layer], o_w_in[j], o_q_norm[j], o_k_norm[j],
                            o_sinks[j], o_w_gate_b[j], o_gate_bias[j], o_gla_norm[j], o_w_out[j])
        x2 = _ffn(x2, post_norm[layer], post_w_gate[layer], post_w_up[layer], post_w_down[layer])
    return x2.reshape(b, s, d)
```

```python
import functools
import math

import jax
import jax.numpy as jnp
import numpy as np
from jax import lax
from jax.experimental import pallas as pl
from jax.experimental.pallas import tpu as pltpu

F32 = jnp.float32
BF16 = jnp.bfloat16
HIGHEST = lax.Precision.HIGHEST

D_MODEL = 1024
D_FF = 2816
FFN_RES = 0.5
EPS = 1e-6
SSM_HEADS = 16
SSM_HEAD_DIM = 64
SSM_D = SSM_HEADS * SSM_HEAD_DIM
SSM_GROUPS = 4
SSM_STATE = 128
SSM_CONV = 4
SSM_CHUNK = 128
SSM_XBC = SSM_D + 2 * SSM_GROUPS * SSM_STATE
MLA_HEADS = 8
MLA_Q_RANK = 384
MLA_KV_RANK = 256
MLA_NOPE = 64
MLA_ROPE = 32
MLA_QK = MLA_NOPE + MLA_ROPE
MLA_V = 64
ROPE_THETA = 10000.0
SWA_Q_HEADS = 8
SWA_KV_HEADS = 2
SWA_HD = 64
SWA_WINDOW = 128
GLA_HEADS = 4
GLA_DK = 64
GLA_DV = 128
GLA_RANK = 16
GLA_TAU = 16.0
GLA_CHUNK = 64

LANES = 128
VMEM_LIMIT_BYTES = 56 * 1024 * 1024

FF_CHUNK = 256
TOKEN_TILE = 512
FFN_TILE = 1024
MLA_BLOCK = 256
SWA_TILE = 512
GLA_TILE = 256
SSD_HISTORY = 16
SSD_WINDOW = 256
SSD_CHUNKS_PER_STEP = 2


def _params(*sem):
    return pltpu.CompilerParams(dimension_semantics=sem, vmem_limit_bytes=VMEM_LIMIT_BYTES)


def _resident(shape):
    nd = len(shape)
    return pl.BlockSpec(shape, lambda *_: (0,) * nd, pipeline_mode=pl.Buffered(1))


def _rms(x, gain):
    return x * lax.rsqrt(jnp.mean(x * x, axis=-1, keepdims=True) + EPS) * gain


def _silu(x):
    return x * jax.nn.sigmoid(x)


def _dot(a, b, **kw):
    return jnp.dot(a, b, preferred_element_type=F32, **kw)


def _dot_nt(a, b):
    return lax.dot_general(a, b, (((1,), (1,)), ((), ())), preferred_element_type=F32)


def _split_bf16(x, n):
    parts = []
    for _ in range(n - 1):
        p = x.astype(BF16)
        parts.append(p)
        x = x - p.astype(F32)
    parts.append(x.astype(BF16))
    return parts


def _dot_tn(a, b):
    return lax.dot_general(a, b, (((0,), (0,)), ((), ())), preferred_element_type=F32)


def _ffn_body(x_ref, g_ref, wgu_ref, wd_ref, o_ref, h_ref, acc_ref, *, n_chunks):
    h_ref[...] = _rms(x_ref[...], g_ref[...]).astype(BF16)
    acc_ref[...] = jnp.zeros_like(acc_ref)

    def step(c, carry):
        gu = _dot(h_ref[...], wgu_ref[c])
        act = (_silu(gu[:, :FF_CHUNK]) * gu[:, FF_CHUNK:]).astype(BF16)
        acc_ref[...] += _dot(act, wd_ref[c])
        return carry

    lax.fori_loop(0, n_chunks, step, 0)
    o_ref[...] = x_ref[...] + acc_ref[...]


def _ffn(x2, norm, w_gate, w_up, w_down):
    t = x2.shape[0]
    tm = min(FFN_TILE, t)
    n_chunks = D_FF // FF_CHUNK
    wg = w_gate.reshape(D_MODEL, n_chunks, FF_CHUNK).transpose(1, 0, 2)
    wu = w_up.reshape(D_MODEL, n_chunks, FF_CHUNK).transpose(1, 0, 2)
    wgu = jnp.concatenate([wg, wu], axis=-1).astype(BF16)
    wd = (FFN_RES * w_down).reshape(n_chunks, FF_CHUNK, D_MODEL).astype(BF16)
    return pl.pallas_call(
        functools.partial(_ffn_body, n_chunks=n_chunks),
        out_shape=jax.ShapeDtypeStruct((t, D_MODEL), F32),
        grid=(t // tm,),
        in_specs=[
            pl.BlockSpec((tm, D_MODEL), lambda i: (i, 0)),
            _resident((1, D_MODEL)),
            _resident(wgu.shape),
            _resident(wd.shape),
        ],
        out_specs=pl.BlockSpec((tm, D_MODEL), lambda i: (i, 0)),
        scratch_shapes=[pltpu.VMEM((tm, D_MODEL), BF16), pltpu.VMEM((tm, D_MODEL), F32)],
        compiler_params=_params("parallel"),
        name="swiglu_half_step",
    )(x2, norm.reshape(1, D_MODEL), wgu, wd)


def _head_norm(t, gain, n_valid):
    ms = jnp.sum(t * t, axis=-1, keepdims=True) * (1.0 / n_valid)
    return t * lax.rsqrt(ms + EPS) * gain


def _inproj_body(x_ref, g_ref, *refs, head_norms):
    n_out = len(head_norms)
    w_refs = refs[:n_out]
    n_gain = sum(1 for hn in head_norms if hn)
    gain_refs = refs[n_out:n_out + n_gain]
    o_refs = refs[n_out + n_gain:]
    h = _rms(x_ref[...], g_ref[...]).astype(BF16)
    gi = 0
    for w_ref, o_ref, hn in zip(w_refs, o_refs, head_norms):
        y = _dot(h, w_ref[...])
        if hn:
            gain = gain_refs[gi][...]
            gi += 1
            for hd in range(y.shape[1] // LANES):
                sl = slice(hd * LANES, (hd + 1) * LANES)
                o_ref[:, sl] = _head_norm(y[:, sl], gain, hn).astype(o_ref.dtype)
        else:
            o_ref[...] = y.astype(o_ref.dtype)


def _inproj(x2, norm, weights, out_dtypes, head_norms, gains):
    t = x2.shape[0]
    tm = min(TOKEN_TILE, t)
    in_specs = [pl.BlockSpec((tm, D_MODEL), lambda i: (i, 0)), _resident((1, D_MODEL))]
    in_specs += [_resident(w.shape) for w in weights]
    in_specs += [_resident(g.shape) for g in gains]
    return pl.pallas_call(
        functools.partial(_inproj_body, head_norms=tuple(head_norms)),
        out_shape=[jax.ShapeDtypeStruct((t, w.shape[1]), dt) for w, dt in zip(weights, out_dtypes)],
        grid=(t // tm,),
        in_specs=in_specs,
        out_specs=[pl.BlockSpec((tm, w.shape[1]), lambda i: (i, 0)) for w in weights],
        compiler_params=_params("parallel"),
        name="mixer_in_proj",
    )(x2, norm.reshape(1, D_MODEL), *weights, *gains)


def _outproj_body(x_ref, *refs):
    n_in = (len(refs) - 1) // 2
    acc = x_ref[...]
    for a_ref, w_ref in zip(refs[:n_in], refs[n_in:2 * n_in]):
        acc = acc + _dot(a_ref[...], w_ref[...])
    refs[-1][...] = acc


def _outproj(x2, acts, weights):
    t = x2.shape[0]
    tm = min(TOKEN_TILE, t)
    in_specs = [pl.BlockSpec((tm, D_MODEL), lambda i: (i, 0))]
    in_specs += [pl.BlockSpec((tm, a.shape[1]), lambda i: (i, 0)) for a in acts]
    in_specs += [_resident(w.shape) for w in weights]
    return pl.pallas_call(
        _outproj_body,
        out_shape=jax.ShapeDtypeStruct((t, D_MODEL), F32),
        grid=(t // tm,),
        in_specs=in_specs,
        out_specs=pl.BlockSpec((tm, D_MODEL), lambda i: (i, 0)),
        compiler_params=_params("parallel"),
        name="mixer_out_proj",
    )(x2, *acts, *weights)


def _rope_body(pos_ref, inv_ref, cos_ref, sin_ref):
    ang = pos_ref[...] * inv_ref[...]
    cos_ref[...] = jnp.cos(ang)
    sin_ref[...] = jnp.sin(ang)


def _rope_tables(positions):
    b, s = positions.shape
    half = MLA_ROPE // 2
    inv = ROPE_THETA ** (-jnp.arange(0, MLA_ROPE, 2, dtype=F32) / MLA_ROPE)
    per_row = LANES // half
    rows = b * s // per_row
    pos_rep = jnp.repeat(positions.astype(F32).reshape(rows, per_row), half, axis=1)
    inv_row = jnp.tile(inv, per_row).reshape(1, LANES)
    tr = min(1024, rows)
    cos_c, sin_c = pl.pallas_call(
        _rope_body,
        out_shape=[jax.ShapeDtypeStruct((rows, LANES), F32)] * 2,
        grid=(rows // tr,),
        in_specs=[pl.BlockSpec((tr, LANES), lambda i: (i, 0)), _resident((1, LANES))],
        out_specs=[pl.BlockSpec((tr, LANES), lambda i: (i, 0))] * 2,
        compiler_params=_params("parallel"),
        name="rope_tables",
    )(pos_rep, inv_row)
    cos = cos_c.reshape(b * s, half)
    sin = sin_c.reshape(b * s, half)
    ones = jnp.ones((b * s, MLA_NOPE), F32)
    tail = jnp.ones((b * s, LANES - MLA_QK), F32)
    cos_full = jnp.concatenate([ones, cos, cos, tail], axis=1)
    sin_full = jnp.concatenate([0 * ones, -sin, sin, 0 * tail], axis=1)
    return cos_full, sin_full


def _mla_prep_body(qa_ref, kvc_ref, kpe_ref, cos_ref, sin_ref, qan_ref, kvn_ref, wq_ref, wk_ref,
                   wv_ref, qn_ref, kn_ref, q_ref, k_ref, v_ref):
    hw = MLA_HEADS * LANES
    cosf = cos_ref[...]
    sinf = sin_ref[...]
    qscale = MLA_QK ** -0.5 * math.log2(math.e)
    q_cos = (qn_ref[:, :LANES] * qscale) * cosf
    q_sin = (qn_ref[:, LANES:] * qscale) * sinf
    k_cos = kn_ref[:, :LANES] * cosf
    kpe = kpe_ref[:, :LANES]
    k_rot = kpe_ref[:, LANES:] * (kn_ref[:, LANES:] * sinf)

    def inv_rms(t):
        return lax.rsqrt(jnp.sum(t * t, axis=-1, keepdims=True) * (1.0 / MLA_QK) + EPS)

    qa = _rms(qa_ref[...].astype(F32), qan_ref[...]).astype(BF16)
    q_all = _dot(qa, wq_ref[...])
    kvc = _rms(kvc_ref[...].astype(F32), kvn_ref[...]).astype(BF16)
    k_all = _dot(kvc, wk_ref[...])
    v_ref[...] = _dot(kvc, wv_ref[...]).astype(v_ref.dtype)
    for hd in range(MLA_HEADS):
        sl = slice(hd * LANES, (hd + 1) * LANES)
        yq = q_all[:, sl]
        yq_rot = q_all[:, hw + hd * LANES:hw + (hd + 1) * LANES]
        q_ref[:, sl] = ((yq * q_cos + yq_rot * q_sin) * inv_rms(yq)).astype(q_ref.dtype)
        yk = k_all[:, sl] + kpe
        k_ref[:, sl] = ((yk * k_cos + k_rot) * inv_rms(yk)).astype(k_ref.dtype)


def _mla_prep(qa, kvc, kpe, cos_full, sin_full, qan, kvn, wq, wk, wv, qn, kn):
    t = qa.shape[0]
    tm = min(TOKEN_TILE, t)
    row = lambda n: pl.BlockSpec((tm, n), lambda i: (i, 0))
    hw = MLA_HEADS * LANES
    return pl.pallas_call(
        _mla_prep_body,
        out_shape=[jax.ShapeDtypeStruct((t, hw), BF16), jax.ShapeDtypeStruct((t, hw), BF16),
                   jax.ShapeDtypeStruct((t, MLA_HEADS * MLA_V), BF16)],
        grid=(t // tm,),
        in_specs=[row(MLA_Q_RANK), row(MLA_KV_RANK), row(2 * LANES), row(LANES), row(LANES),
                  _resident(qan.shape), _resident(kvn.shape), _resident(wq.shape),
                  _resident(wk.shape), _resident(wv.shape), _resident(qn.shape), _resident(kn.shape)],
        out_specs=[row(hw), row(hw), row(MLA_HEADS * MLA_V)],
        compiler_params=_params("parallel"),
        name="mla_prep",
    )(qa, kvc, kpe, cos_full, sin_full, qan, kvn, wq, wk, wv, qn, kn)


def _mla_attn_body(q_ref, k_ref, v_ref, o_ref, *, blk, n_blk):
    row = lax.broadcasted_iota(jnp.int32, (blk, blk), 0)
    col = lax.broadcasted_iota(jnp.int32, (blk, blk), 1)
    causal = col <= row
    lane = lax.broadcasted_iota(jnp.int32, (blk, LANES), 1)
    for i in range(n_blk):
        rows = slice(i * blk, (i + 1) * blk)
        past = slice(0, i * blk)
        outs = []
        for hd in range(2):
            sl = slice(hd * LANES, (hd + 1) * LANES)
            q = q_ref[0, rows, sl]
            s_d = jnp.where(causal, _dot_nt(q, k_ref[0, rows, sl]), -jnp.inf)
            m = jnp.max(s_d, axis=-1, keepdims=True)
            if i:
                s_p = _dot_nt(q, k_ref[0, past, sl])
                m = jnp.maximum(m, jnp.max(s_p, axis=-1, keepdims=True))
            p_d = jnp.exp2(s_d - m)
            den = jnp.sum(p_d, axis=-1, keepdims=True)
            acc = _dot(p_d.astype(BF16), v_ref[0, rows, :])
            if i:
                p_p = jnp.exp2(s_p - m)
                den = den + jnp.sum(p_p, axis=-1, keepdims=True)
                acc = acc + _dot(p_p.astype(BF16), v_ref[0, past, :])
            outs.append(acc / den)
        o_ref[0, rows, :] = jnp.where(lane < MLA_V, outs[0], outs[1]).astype(o_ref.dtype)


def _mla_attention(q, k, v):
    b, s, _ = q.shape
    blk = min(MLA_BLOCK, s)
    pairs = MLA_HEADS // 2
    return pl.pallas_call(
        functools.partial(_mla_attn_body, blk=blk, n_blk=s // blk),
        out_shape=jax.ShapeDtypeStruct((b, s, MLA_HEADS * MLA_V), BF16),
        grid=(b, pairs),
        in_specs=[
            pl.BlockSpec((1, s, 2 * LANES), lambda bb, p: (bb, 0, p)),
            pl.BlockSpec((1, s, 2 * LANES), lambda bb, p: (bb, 0, p)),
            pl.BlockSpec((1, s, LANES), lambda bb, p: (bb, 0, p)),
        ],
        out_specs=pl.BlockSpec((1, s, LANES), lambda bb, p: (bb, 0, p)),
        compiler_params=_params("parallel", "parallel"),
        name="mla_causal_attention",
    )(q, k, v)


def _softplus(x):
    return jnp.maximum(x, 0.0) + jnp.log(1.0 + jnp.exp(-jnp.abs(x)))


def _ssd_body(xbc_ref, z_ref, dt_ref, cw_ref, cb_ref, dtb_ref, alog_ref, dsk_ref, nrm_ref,
              e64_ref, shift_ref, y_ref, xpad_ref, st_ref, *, n_chunks):
    L = SSM_CHUNK
    pad = SSD_HISTORY

    @pl.when(pl.program_id(1) == 0)
    def _():
        xpad_ref[...] = jnp.zeros_like(xpad_ref)
        st_ref[...] = jnp.zeros_like(st_ref)

    xpad_ref[pad:pad + n_chunks * L, :] = xbc_ref[0]
    convs = []
    for ci in range(n_chunks):
        taps = _dot(shift_ref[...], xpad_ref[ci * L:ci * L + SSD_WINDOW, :])
        conv = cb_ref[...]
        for back in range(SSM_CONV):
            tap = SSM_CONV - 1 - back
            conv = conv + cw_ref[tap:tap + 1, :] * taps[back * L:(back + 1) * L]
        convs.append(conv)
    xpad_ref[0:pad, :] = xpad_ref[n_chunks * L:n_chunks * L + pad, :]
    for ci in range(n_chunks):
        _ssd_chunk(convs[ci], slice(ci * L, (ci + 1) * L), z_ref, dt_ref, dtb_ref, alog_ref, dsk_ref,
                   nrm_ref, e64_ref, y_ref, st_ref)


def _ssd_chunk(conv, rows, z_ref, dt_ref, dtb_ref, alog_ref, dsk_ref, nrm_ref, e64_ref, y_ref, st_ref):
    L = SSM_CHUNK
    act = _silu(conv)
    xs = act[:, :SSM_D]
    gn = SSM_GROUPS * SSM_STATE
    b_all = act[:, SSM_D:SSM_D + gn]
    c_all = act[:, SSM_D + gn:]

    lane = lax.broadcasted_iota(jnp.int32, (L, LANES), 1)
    row = lax.broadcasted_iota(jnp.int32, (L, L), 0)
    col = lax.broadcasted_iota(jnp.int32, (L, L), 1)
    dt = jnp.where(lane < SSM_HEADS, _softplus(dt_ref[0, rows] + dtb_ref[...]), 0.0)
    a = dt * (-math.log2(math.e) * jnp.exp(alog_ref[...]))
    a_parts = _split_bf16(a, 3)
    tri = jnp.where(col <= row, 1.0, 0.0).astype(BF16)
    cs3 = _dot(tri, jnp.concatenate(a_parts, axis=1))
    cs = cs3[:, :LANES] + cs3[:, LANES:2 * LANES] + cs3[:, 2 * LANES:]
    cs_t = cs.T
    cs_last = cs[L - 1:L, :]
    ecs = jnp.exp2(cs)
    stacked = jnp.concatenate([dt, ecs, jnp.exp2(cs_last - cs)], axis=0)
    wide = _dot(jnp.concatenate(_split_bf16(stacked, 2), axis=1), e64_ref[...])
    dt_w, ecs_w, dec_w = wide[:L], wide[L:2 * L], wide[2 * L:]

    xd = xs * dt_w
    xd_b = xd.astype(BF16)
    xdw_b = (xd * dec_w).astype(BF16)
    chunk_decay = ecs_w[L - 1:L, :]

    heads_per_group = SSM_HEADS // SSM_GROUPS
    gw = heads_per_group * SSM_HEAD_DIM
    cs_col = jnp.concatenate([jnp.broadcast_to(cs[:, h:h + 1], (L, L)) for h in range(SSM_HEADS)], axis=1)
    cs_row = jnp.concatenate([jnp.broadcast_to(cs_t[h:h + 1, :], (L, L)) for h in range(SSM_HEADS)], axis=1)
    decay = jnp.exp2(jnp.minimum(cs_col - cs_row, 0.0))
    b_bf = b_all.astype(BF16)
    c_bf = c_all.astype(BF16)
    cbs = [_dot_nt(c_bf[:, g * SSM_STATE:(g + 1) * SSM_STATE], b_bf[:, g * SSM_STATE:(g + 1) * SSM_STATE])
           for g in range(SSM_GROUPS)]
    cbs = [jnp.where(col <= row, cb, 0.0) for cb in cbs]
    cb_all = jnp.concatenate([cbs[h // heads_per_group] for h in range(SSM_HEADS)], axis=1)
    mix = (cb_all * decay).astype(BF16)

    y_parts = []
    for pi in range(SSM_HEADS // 2):
        pair = xd_b[:, pi * LANES:(pi + 1) * LANES]
        zero = jnp.zeros_like(pair)
        rhs = jnp.concatenate([jnp.where(lane < SSM_HEAD_DIM, pair, zero),
                               jnp.where(lane < SSM_HEAD_DIM, zero, pair)], axis=0)
        y_parts.append(_dot(mix[:, 2 * pi * L:(2 * pi + 2) * L], rhs))
    yoff_parts = []
    for g in range(SSM_GROUPS):
        st_g = st_ref[:, g * gw:(g + 1) * gw]
        yoff_parts.append(_dot(c_bf[:, g * SSM_STATE:(g + 1) * SSM_STATE], st_g.astype(BF16)))
        bg_t = b_all[:, g * SSM_STATE:(g + 1) * SSM_STATE].T.astype(BF16)
        upd = _dot(bg_t, xdw_b[:, g * gw:(g + 1) * gw])
        st_ref[:, g * gw:(g + 1) * gw] = st_g * chunk_decay[:, g * gw:(g + 1) * gw] + upd

    y = jnp.concatenate(y_parts, axis=1) + ecs_w * jnp.concatenate(yoff_parts, axis=1)
    y = (y + dsk_ref[...] * xs) * _silu(z_ref[0, rows].astype(F32))
    for g in range(SSM_GROUPS):
        sl = slice(g * gw, (g + 1) * gw)
        y_ref[0, rows, sl] = _rms(y[:, sl], nrm_ref[:, sl]).astype(y_ref.dtype)


def _ssd(xbc, z, dt, conv_w, conv_b, dt_bias, a_log, d_skip, ssm_norm):
    b, s, _ = xbc.shape
    L = SSM_CHUNK
    heads = np.arange(2 * LANES)[:, None] % LANES
    e64 = jnp.asarray(heads == (np.arange(SSM_D)[None, :] // SSM_HEAD_DIM), BF16)
    t_idx = np.arange(SSM_CONV * L)
    shift = np.zeros((SSM_CONV * L, SSD_WINDOW), np.float32)
    shift[t_idx, SSD_HISTORY + t_idx % L - t_idx // L] = 1.0
    shift = jnp.asarray(shift, BF16)
    padh = lambda v: jnp.pad(v.reshape(1, SSM_HEADS), ((0, 0), (0, LANES - SSM_HEADS)))
    dsk = jnp.repeat(d_skip, SSM_HEAD_DIM).reshape(1, SSM_D)
    nc = SSD_CHUNKS_PER_STEP
    blk = lambda n: pl.BlockSpec((1, nc * L, n), lambda bb, c: (bb, c, 0))
    return pl.pallas_call(
        functools.partial(_ssd_body, n_chunks=nc),
        out_shape=jax.ShapeDtypeStruct((b, s, SSM_D), BF16),
        grid=(b, s // (nc * L)),
        in_specs=[blk(SSM_XBC), blk(SSM_D), blk(LANES),
                  _resident((SSM_CONV, SSM_XBC)), _resident((1, SSM_XBC)), _resident((1, LANES)),
                  _resident((1, LANES)), _resident((1, SSM_D)), _resident((1, SSM_D)),
                  _resident(e64.shape), _resident(shift.shape)],
        out_specs=blk(SSM_D),
        scratch_shapes=[pltpu.VMEM(((nc - 1) * L + SSD_WINDOW, SSM_XBC), BF16), pltpu.VMEM((SSM_STATE, SSM_D), F32)],
        compiler_params=_params("parallel", "arbitrary"),
        name="ssd_scan",
    )(xbc, z, dt, conv_w, conv_b.reshape(1, SSM_XBC), padh(dt_bias), padh(a_log), dsk,
      ssm_norm.reshape(1, SSM_D), e64, shift)


def _swa_body(q_ref, k_ref, v_ref, pc_ref, pr_ref, sink_ref, o_ref, *, tq):
    W = SWA_WINDOW
    grp = SWA_Q_HEADS // SWA_KV_HEADS
    log2e = math.log2(math.e)
    slopes = [log2e * 2.0 ** (-8.0 * (h + 1) / SWA_Q_HEADS) for h in range(SWA_Q_HEADS)]
    i = pl.program_id(1)
    lane = lax.broadcasted_iota(jnp.int32, (W, LANES), 1)
    qi = lax.broadcasted_iota(jnp.int32, (W, 2 * W), 0)
    ki = lax.broadcasted_iota(jnp.int32, (W, 2 * W), 1)
    sinks = sink_ref[...] * log2e
    sink_col = jnp.concatenate(
        [jnp.broadcast_to(sinks[0:1, h:h + 1], (W, 1)) for h in range(SWA_Q_HEADS)], axis=0)
    for jb in range(tq // W):
        j = i * (tq // W) + jb
        kb0 = jnp.maximum(j - 1, 0)
        ks = pl.multiple_of(kb0 * W, W)
        rel = (j - kb0) * W + qi - ki
        valid = (rel >= 0) & (rel < W)
        pq = pc_ref[0, jb * W:(jb + 1) * W, :]
        pk = jnp.concatenate([pr_ref[0, kb0], pr_ref[0, kb0 + 1]], axis=1)
        dist = jnp.where(valid, jnp.abs(pq - pk).astype(F32), jnp.inf)
        bias = jnp.concatenate([slopes[h] * dist for h in range(SWA_Q_HEADS)], axis=0)
        rows = slice(jb * W, (jb + 1) * W)
        logits = []
        for kv in range(SWA_KV_HEADS):
            kk = k_ref[0, pl.ds(ks, 2 * W), kv * LANES:(kv + 1) * LANES]
            qs = jnp.concatenate(
                [q_ref[0, rows, (kv * grp + g) * LANES:(kv * grp + g + 1) * LANES] for g in range(grp)],
                axis=0)
            logits.append(_dot_nt(qs, kk))
        s = jnp.concatenate(logits, axis=0) - bias
        m = jnp.maximum(jnp.max(s, axis=-1, keepdims=True), sink_col)
        p = jnp.exp2(s - m)
        den = jnp.sum(p, axis=-1, keepdims=True) + jnp.exp2(sink_col - m)
        pb = p.astype(BF16)
        outs = []
        for kv in range(SWA_KV_HEADS):
            vv = v_ref[0, pl.ds(ks, 2 * W), kv * LANES:(kv + 1) * LANES]
            outs.append(_dot(pb[kv * grp * W:(kv + 1) * grp * W], vv))
        o = jnp.concatenate(outs, axis=0) / den
        for pr in range(SWA_Q_HEADS // 2):
            o_pair = jnp.where(lane < SWA_HD, o[2 * pr * W:(2 * pr + 1) * W], o[(2 * pr + 1) * W:(2 * pr + 2) * W])
            o_ref[0, rows, pr * LANES:(pr + 1) * LANES] = o_pair.astype(o_ref.dtype)


def _swa(q, k, v, pos, sinks):
    b, s, _ = q.shape
    tq = min(SWA_TILE, s)
    W = SWA_WINDOW
    nb = s // W
    pos_col = pos.reshape(b, s, 1)
    pos_row = jnp.concatenate([pos.reshape(b, nb, 1, W), jnp.zeros((b, 1, 1, W), pos.dtype)], axis=1)
    return pl.pallas_call(
        functools.partial(_swa_body, tq=tq),
        out_shape=jax.ShapeDtypeStruct((b, s, SWA_Q_HEADS * SWA_HD), BF16),
        grid=(b, s // tq),
        in_specs=[
            pl.BlockSpec((1, tq, SWA_Q_HEADS * LANES), lambda bb, i: (bb, i, 0)),
            pl.BlockSpec((1, s, SWA_KV_HEADS * LANES), lambda bb, i: (bb, 0, 0)),
            pl.BlockSpec((1, s, SWA_KV_HEADS * LANES), lambda bb, i: (bb, 0, 0)),
            pl.BlockSpec((1, tq, 1), lambda bb, i: (bb, i, 0)),
            pl.BlockSpec((1, nb + 1, 1, W), lambda bb, i: (bb, 0, 0, 0)),
            _resident((1, SWA_Q_HEADS)),
        ],
        out_specs=pl.BlockSpec((1, tq, SWA_Q_HEADS * SWA_HD), lambda bb, i: (bb, i, 0)),
        compiler_params=_params("parallel", "parallel"),
        name="swa_sink_attention",
    )(q, k, v, pos_col, pos_row, sinks.reshape(1, SWA_Q_HEADS))


def _log_sigmoid(x):
    return jnp.minimum(x, 0.0) - jnp.log(1.0 + jnp.exp(-jnp.abs(x)))


def _gla_body(q_ref, k_ref, v_ref, ga_ref, gr_ref, wgb_ref, gb_ref, nrm_ref, o_ref, st_ref, *, tg):
    C = GLA_CHUNK

    @pl.when(pl.program_id(1) == 0)
    def _():
        st_ref[...] = jnp.zeros_like(st_ref)

    row = lax.broadcasted_iota(jnp.int32, (tg, tg), 0)
    col = lax.broadcasted_iota(jnp.int32, (tg, tg), 1)
    shift = C.bit_length() - 1
    same_chunk = lax.shift_right_logical(row, shift) == lax.shift_right_logical(col, shift)
    intra = same_chunk & (col <= row)
    hw = GLA_HEADS * LANES
    g = _log_sigmoid(_dot(ga_ref[0], wgb_ref[...]) + gb_ref[...]) * (math.log2(math.e) / GLA_TAU)
    masks = jnp.concatenate([jnp.where(intra, 1.0, 0.0), jnp.where(same_chunk, 1.0, 0.0)], axis=0).astype(BF16)
    sums = _dot(masks, jnp.concatenate(_split_bf16(g, 2), axis=1))
    sums = sums[:, :hw] + sums[:, hw:]
    bcum, btot = sums[:tg], sums[tg:]
    q_dec = (q_ref[0].astype(F32) * jnp.exp2(bcum)).astype(BF16)
    kf = k_ref[0].astype(F32)
    k_inv = (kf * jnp.exp2(-bcum)).astype(BF16)
    k_end = (kf * jnp.exp2(btot - bcum)).astype(BF16)
    chunk_dec = jnp.exp2(btot)
    n_c = tg // C
    heads = [slice(hd * LANES, (hd + 1) * LANES) for hd in range(GLA_HEADS)]
    chunks = [slice(c * C, (c + 1) * C) for c in range(n_c)]
    vs = [v_ref[0, :, sl] for sl in heads]
    o_intra = [_dot(jnp.where(intra, _dot_nt(q_dec[:, sl], k_inv[:, sl]), 0.0).astype(BF16), vh)
               for sl, vh in zip(heads, vs)]
    kvs = [[_dot_tn(vh[rows], k_end[rows, sl]) for rows in chunks] for sl, vh in zip(heads, vs)]
    for hd, sl in enumerate(heads):
        st = st_ref[hd]
        inter = []
        for c, rows in enumerate(chunks):
            inter.append(_dot_nt(q_dec[rows, sl], st.astype(BF16)))
            st = st * chunk_dec[c * C:c * C + 1, sl] + kvs[hd][c]
        st_ref[hd] = st
        o = o_intra[hd] + jnp.concatenate(inter, axis=0)
        o = _rms(o, nrm_ref[...]) * _silu(gr_ref[0, :, sl].astype(F32))
        o_ref[0, :, sl] = o.astype(o_ref.dtype)


def _gla(q, k, v, ga, gr, wgb, gate_bias, gla_norm):
    b, s, _ = q.shape
    tg = min(GLA_TILE, s)
    hw = GLA_HEADS * LANES
    blk = lambda n: pl.BlockSpec((1, tg, n), lambda bb, i: (bb, i, 0))
    return pl.pallas_call(
        functools.partial(_gla_body, tg=tg),
        out_shape=jax.ShapeDtypeStruct((b, s, GLA_HEADS * GLA_DV), BF16),
        grid=(b, s // tg),
        in_specs=[blk(hw), blk(hw), blk(GLA_HEADS * GLA_DV), blk(LANES), blk(GLA_HEADS * GLA_DV),
                  _resident(wgb.shape), _resident((1, hw)), _resident((1, GLA_DV))],
        out_specs=blk(GLA_HEADS * GLA_DV),
        scratch_shapes=[pltpu.VMEM((GLA_HEADS, GLA_DV, LANES), F32)],
        compiler_params=_params("parallel", "arbitrary"),
        name="gla_chunked",
    )(q, k, v, ga, gr, wgb, gate_bias, gla_norm.reshape(1, GLA_DV))


def _pad_heads(w, n_heads, width, lanes=LANES, offset=0):
    lead = w.shape[:-1]
    w = w.reshape(lead + (n_heads, width))
    cfg = [(0, 0)] * len(lead) + [(0, 0), (offset, lanes - width - offset)]
    return jnp.pad(w, cfg).reshape(lead + (n_heads * lanes,))


def _rot_cols(w):
    half = MLA_ROPE // 2
    lane = np.arange(LANES)
    src = np.where((lane >= MLA_NOPE) & (lane < MLA_NOPE + half), lane + half,
                   np.where((lane >= MLA_NOPE + half) & (lane < MLA_QK), lane - half, 0))
    keep = jnp.asarray((lane >= MLA_NOPE) & (lane < MLA_QK), w.dtype)
    blocks = w.reshape(w.shape[:-1] + (w.shape[-1] // LANES, LANES))
    return (blocks[..., src] * keep).reshape(w.shape)


def _pad_cols(w, total, offset=0):
    return jnp.pad(w, [(0, 0)] * (w.ndim - 1) + [(offset, total - w.shape[-1] - offset)])


def _even_mixer(x2, b, s, pos, mix_norm, w_in, conv_w, conv_b, dt_bias, a_log, d_skip, ssm_norm,
                q_a_norm, w_q_b, kv_a_norm, w_kv_b, q_norm, k_norm, w_out):
    t = b * s
    o0 = 0
    cols = {}
    for name, n in (("z", SSM_D), ("xbc", SSM_XBC), ("dt", SSM_HEADS), ("qa", MLA_Q_RANK),
                    ("kvc", MLA_KV_RANK), ("kpe", MLA_ROPE)):
        cols[name] = w_in[:, o0:o0 + n]
        o0 += n
    w_kpe = _pad_cols(cols["kpe"], LANES, offset=MLA_NOPE)
    weights = [cols["z"], cols["xbc"], _pad_cols(cols["dt"], LANES), cols["qa"], cols["kvc"],
               jnp.concatenate([w_kpe, _rot_cols(w_kpe)], axis=1)]
    weights = [w.astype(BF16) for w in weights]
    z, xbc, dt, qa, kvc, kpe = _inproj(x2, mix_norm, weights, [BF16, BF16, F32, BF16, BF16, F32],
                                       [0] * 6, [])
    y = _ssd(xbc.reshape(b, s, -1), z.reshape(b, s, -1), dt.reshape(b, s, -1), conv_w, conv_b,
             dt_bias, a_log, d_skip, ssm_norm).reshape(t, SSM_D)

    cos_full, sin_full = _rope_tables(pos)
    wq = _pad_heads(w_q_b, MLA_HEADS, MLA_QK)
    wq = jnp.concatenate([wq, _rot_cols(wq)], axis=1).astype(BF16)
    with_rot = lambda g: jnp.concatenate([g, _rot_cols(g)], axis=1)
    w_kv = w_kv_b.reshape(MLA_KV_RANK, MLA_HEADS, MLA_NOPE + MLA_V)
    wk = _pad_heads(w_kv[:, :, :MLA_NOPE].reshape(MLA_KV_RANK, -1), MLA_HEADS, MLA_NOPE).astype(BF16)
    wv = w_kv[:, :, MLA_NOPE:].reshape(MLA_KV_RANK, MLA_HEADS * MLA_V).astype(BF16)
    q, k, v = _mla_prep(qa, kvc, kpe, cos_full, sin_full, q_a_norm.reshape(1, -1),
                        kv_a_norm.reshape(1, -1), wq, wk, wv,
                        with_rot(_pad_cols(q_norm.reshape(1, -1), LANES)),
                        with_rot(_pad_cols(k_norm.reshape(1, -1), LANES)))
    o = _mla_attention(q.reshape(b, s, -1), k.reshape(b, s, -1), v.reshape(b, s, -1)).reshape(t, -1)
    return _outproj(x2, [y, o], [w_out[:SSM_D].astype(BF16), w_out[SSM_D:].astype(BF16)])


def _odd_mixer(x2, b, s, pos, mix_norm, w_in, q_norm, k_norm, sinks, w_gate_b, gate_bias, gla_norm,
               w_out):
    t = b * s
    sizes = [SWA_Q_HEADS * SWA_HD, SWA_KV_HEADS * SWA_HD, SWA_KV_HEADS * SWA_HD,
             GLA_HEADS * GLA_DK, GLA_HEADS * GLA_DK, GLA_HEADS * GLA_DV, GLA_RANK, GLA_HEADS * GLA_DV]
    parts, o0 = [], 0
    for n in sizes:
        parts.append(w_in[:, o0:o0 + n])
        o0 += n
    wq, wk, wv, wgq, wgk, wgv, wga, wgr = parts
    v_dup = jnp.repeat(wv.reshape(D_MODEL, SWA_KV_HEADS, 1, SWA_HD), 2, axis=2).reshape(D_MODEL, -1)
    weights = [_pad_heads(wq, SWA_Q_HEADS, SWA_HD), _pad_heads(wk, SWA_KV_HEADS, SWA_HD), v_dup,
               _pad_heads(wgq * (GLA_DK ** -0.5), GLA_HEADS, GLA_DK), _pad_heads(wgk, GLA_HEADS, GLA_DK),
               wgv, _pad_cols(wga, LANES), wgr]
    weights = [w.astype(BF16) for w in weights]
    gains = [_pad_cols(q_norm.reshape(1, -1) * (SWA_HD ** -0.5 * math.log2(math.e)), LANES),
             _pad_cols(k_norm.reshape(1, -1), LANES)]
    q, k, v, gq, gk, gv, ga, gr = _inproj(x2, mix_norm, weights, [BF16] * 8,
                                          [SWA_HD, SWA_HD, 0, 0, 0, 0, 0, 0], gains)
    o_swa = _swa(q.reshape(b, s, -1), k.reshape(b, s, -1), v.reshape(b, s, -1), pos, sinks)
    wgb = _pad_heads(jnp.pad(w_gate_b, ((0, LANES - GLA_RANK), (0, 0))), GLA_HEADS, GLA_DK).astype(BF16)
    gb = _pad_heads(gate_bias.reshape(1, -1), GLA_HEADS, GLA_DK)
    o_gla = _gla(gq.reshape(b, s, -1), gk.reshape(b, s, -1), gv.reshape(b, s, -1), ga.reshape(b, s, -1),
                 gr.reshape(b, s, -1), wgb, gb, gla_norm)
    n_swa = SWA_Q_HEADS * SWA_HD
    return _outproj(x2, [o_swa.reshape(t, -1), o_gla.reshape(t, -1)],
                    [w_out[:n_swa].astype(BF16), w_out[n_swa:].astype(BF16)])


def kernel(x, positions, pre_norm, pre_w_gate, pre_w_up, pre_w_down, mix_norm, post_norm, post_w_gate,
           post_w_up, post_w_down, e_w_in, e_conv_w, e_conv_b, e_dt_bias, e_a_log, e_d_skip, e_ssm_norm,
           e_q_a_norm, e_w_q_b, e_kv_a_norm, e_w_kv_b, e_q_norm, e_k_norm, e_w_out, o_w_in, o_q_norm,
           o_k_norm, o_sinks, o_w_gate_b, o_gate_bias, o_gla_norm, o_w_out):
    b, s, d = x.shape
    depth = pre_norm.shape[0]
    x2 = x.reshape(b * s, d)
    for layer in range(depth):
        x2 = _ffn(x2, pre_norm[layer], pre_w_gate[layer], pre_w_up[layer], pre_w_down[layer])
        j = layer // 2
        if layer % 2 == 0:
            x2 = _even_mixer(x2, b, s, positions, mix_norm[layer], e_w_in[j], e_conv_w[j], e_conv_b[j],
                             e_dt_bias[j], e_a_log[j], e_d_skip[j], e_ssm_norm[j], e_q_a_norm[j],
                             e_w_q_b[j], e_kv_a_norm[j], e_w_kv_b[j], e_q_norm[j], e_k_norm[j], e_w_out[j])
        else:
            x2 = _odd_mixer(x2, b, s, positions, mix_norm[layer], o_w_in[j], o_q_norm[j], o_k_norm[j],
                            o_sinks[j], o_w_gate_b[j], o_gate_bias[j], o_gla_norm[j], o_w_out[j])
        x2 = _ffn(x2, post_norm[layer], post_w_gate[layer], post_w_up[layer], post_w_down[layer])
    return x2.reshape(b, s, d)
```

```python
import functools
import math

import jax
import jax.numpy as jnp
import numpy as np
from jax import lax
from jax.experimental import pallas as pl
from jax.experimental.pallas import tpu as pltpu

F32 = jnp.float32
BF16 = jnp.bfloat16
HIGHEST = lax.Precision.HIGHEST

D_MODEL = 1024
D_FF = 2816
FFN_RES = 0.5
EPS = 1e-6
SSM_HEADS = 16
SSM_HEAD_DIM = 64
SSM_D = SSM_HEADS * SSM_HEAD_DIM
SSM_GROUPS = 4
SSM_STATE = 128
SSM_CONV = 4
SSM_CHUNK = 128
SSM_XBC = SSM_D + 2 * SSM_GROUPS * SSM_STATE
MLA_HEADS = 8
MLA_Q_RANK = 384
MLA_KV_RANK = 256
MLA_NOPE = 64
MLA_ROPE = 32
MLA_QK = MLA_NOPE + MLA_ROPE
MLA_V = 64
ROPE_THETA = 10000.0
SWA_Q_HEADS = 8
SWA_KV_HEADS = 2
SWA_HD = 64
SWA_WINDOW = 128
GLA_HEADS = 4
GLA_DK = 64
GLA_DV = 128
GLA_RANK = 16
GLA_TAU = 16.0
GLA_CHUNK = 64

LANES = 128
VMEM_LIMIT_BYTES = 56 * 1024 * 1024

FF_CHUNK = 256
TOKEN_TILE = 512
FFN_TILE = 1024
MLA_BLOCK = 512
SWA_TILE = 512
GLA_TILE = 256
SSD_HISTORY = 16
SSD_WINDOW = 256
SSD_CHUNKS_PER_STEP = 2


def _params(*sem):
    return pltpu.CompilerParams(dimension_semantics=sem, vmem_limit_bytes=VMEM_LIMIT_BYTES)


def _resident(shape):
    nd = len(shape)
    return pl.BlockSpec(shape, lambda *_: (0,) * nd, pipeline_mode=pl.Buffered(1))


def _rms(x, gain):
    return x * lax.rsqrt(jnp.mean(x * x, axis=-1, keepdims=True) + EPS) * gain


def _silu(x):
    return x * jax.nn.sigmoid(x)


def _dot(a, b, **kw):
    return jnp.dot(a, b, preferred_element_type=F32, **kw)


def _dot_nt(a, b):
    return lax.dot_general(a, b, (((1,), (1,)), ((), ())), preferred_element_type=F32)


def _split_bf16(x, n):
    parts = []
    for _ in range(n - 1):
        p = x.astype(BF16)
        parts.append(p)
        x = x - p.astype(F32)
    parts.append(x.astype(BF16))
    return parts


def _dot_tn(a, b):
    return lax.dot_general(a, b, (((0,), (0,)), ((), ())), preferred_element_type=F32)


def _ffn_body(x_ref, g_ref, wg_ref, wu_ref, wd_ref, *refs, n_mix):
    mix_a, mix_w = refs[:n_mix], refs[n_mix:2 * n_mix]
    o_ref, h_ref, acc_ref = refs[2 * n_mix:]
    x = x_ref[...]
    if n_mix:
        mix = _dot(mix_a[0][...], mix_w[0][...])
        for a_ref, w_ref in zip(mix_a[1:], mix_w[1:]):
            mix = mix + _dot(a_ref[...], w_ref[...])
        x = x + mix
    acc_ref[...] = x
    h_ref[...] = _rms(acc_ref[...], g_ref[...]).astype(BF16)
    for c in range(D_FF // FF_CHUNK):
        cols = slice(c * FF_CHUNK, (c + 1) * FF_CHUNK)
        gate = _dot(h_ref[...], wg_ref[:, cols])
        up = _dot(h_ref[...], wu_ref[:, cols])
        acc_ref[...] += _dot((_silu(gate) * up).astype(BF16), wd_ref[cols, :])
    o_ref[...] = acc_ref[...]


def _ffn(x2, norm, w_gate, w_up, w_down, mix_acts=(), mix_weights=()):
    t = x2.shape[0]
    tm = min(FFN_TILE, t)
    weights = [w_gate.astype(BF16), w_up.astype(BF16), (FFN_RES * w_down).astype(BF16)]
    mix_weights = [w.astype(BF16) for w in mix_weights]
    row = lambda n: pl.BlockSpec((tm, n), lambda i: (i, 0))
    return pl.pallas_call(
        functools.partial(_ffn_body, n_mix=len(mix_acts)),
        out_shape=jax.ShapeDtypeStruct((t, D_MODEL), F32),
        grid=(t // tm,),
        in_specs=[row(D_MODEL), _resident((1, D_MODEL))] + [_resident(w.shape) for w in weights]
        + [row(a.shape[1]) for a in mix_acts] + [_resident(w.shape) for w in mix_weights],
        out_specs=row(D_MODEL),
        scratch_shapes=[pltpu.VMEM((tm, D_MODEL), BF16), pltpu.VMEM((tm, D_MODEL), F32)],
        compiler_params=_params("parallel"),
        name="swiglu_half_step",
    )(x2, norm.reshape(1, D_MODEL), *weights, *mix_acts, *mix_weights)


def _head_norm(t, gain, n_valid):
    ms = jnp.sum(t * t, axis=-1, keepdims=True) * (1.0 / n_valid)
    return t * lax.rsqrt(ms + EPS) * gain


def _inproj_body(x_ref, g_ref, *refs, head_norms):
    n_out = len(head_norms)
    w_refs = refs[:n_out]
    n_gain = sum(1 for hn in head_norms if hn)
    gain_refs = refs[n_out:n_out + n_gain]
    o_refs = refs[n_out + n_gain:]
    h = _rms(x_ref[...], g_ref[...]).astype(BF16)
    gi = 0
    for w_ref, o_ref, hn in zip(w_refs, o_refs, head_norms):
        y = _dot(h, w_ref[...])
        if hn:
            gain = gain_refs[gi][...]
            gi += 1
            for hd in range(y.shape[1] // LANES):
                sl = slice(hd * LANES, (hd + 1) * LANES)
                o_ref[:, sl] = _head_norm(y[:, sl], gain, hn).astype(o_ref.dtype)
        else:
            o_ref[...] = y.astype(o_ref.dtype)


def _inproj(x2, norm, weights, out_dtypes, head_norms, gains):
    t = x2.shape[0]
    tm = min(TOKEN_TILE, t)
    in_specs = [pl.BlockSpec((tm, D_MODEL), lambda i: (i, 0)), _resident((1, D_MODEL))]
    in_specs += [_resident(w.shape) for w in weights]
    in_specs += [_resident(g.shape) for g in gains]
    return pl.pallas_call(
        functools.partial(_inproj_body, head_norms=tuple(head_norms)),
        out_shape=[jax.ShapeDtypeStruct((t, w.shape[1]), dt) for w, dt in zip(weights, out_dtypes)],
        grid=(t // tm,),
        in_specs=in_specs,
        out_specs=[pl.BlockSpec((tm, w.shape[1]), lambda i: (i, 0)) for w in weights],
        compiler_params=_params("parallel"),
        name="mixer_in_proj",
    )(x2, norm.reshape(1, D_MODEL), *weights, *gains)


def _rope_body(pos_ref, inv_ref, spread_ref, base_ref, cos_ref, sin_ref):
    ang = pos_ref[...] * inv_ref[...]
    terms = _split_bf16(jnp.cos(ang), 3) + _split_bf16(jnp.sin(ang), 3)
    lhs = jnp.concatenate(terms, axis=1)
    for p in range(LANES // (MLA_ROPE // 2)):
        wide = _dot(lhs, spread_ref[p])
        cos_ref[:, p * LANES:(p + 1) * LANES] = wide[:, :LANES] + base_ref[...]
        sin_ref[:, p * LANES:(p + 1) * LANES] = wide[:, LANES:]


def _rope_tables(positions):
    b, s = positions.shape
    half = MLA_ROPE // 2
    inv = ROPE_THETA ** (-jnp.arange(0, MLA_ROPE, 2, dtype=F32) / MLA_ROPE)
    per_row = LANES // half
    rows = b * s // per_row
    pos_rep = jnp.repeat(positions.astype(F32).reshape(rows, per_row), half, axis=1)
    inv_row = jnp.tile(inv, per_row).reshape(1, LANES)
    n_terms = 3
    spread = np.zeros((per_row, 2 * n_terms * LANES, 2 * LANES), np.float32)
    f = np.arange(half)
    for p in range(per_row):
        for part in range(n_terms):
            src_c = part * LANES + p * half + f
            src_s = (n_terms + part) * LANES + p * half + f
            spread[p, src_c, MLA_NOPE + f] = 1.0
            spread[p, src_c, MLA_NOPE + half + f] = 1.0
            spread[p, src_s, LANES + MLA_NOPE + f] = -1.0
            spread[p, src_s, LANES + MLA_NOPE + half + f] = 1.0
    lane = np.arange(LANES)
    base = np.where((lane >= MLA_NOPE) & (lane < MLA_QK), 0.0, 1.0).astype(np.float32).reshape(1, LANES)
    tr = min(1024, rows)
    wide = per_row * LANES
    cos_full, sin_full = pl.pallas_call(
        _rope_body,
        out_shape=[jax.ShapeDtypeStruct((rows, wide), F32)] * 2,
        grid=(rows // tr,),
        in_specs=[pl.BlockSpec((tr, LANES), lambda i: (i, 0)), _resident((1, LANES)),
                  _resident(spread.shape), _resident((1, LANES))],
        out_specs=[pl.BlockSpec((tr, wide), lambda i: (i, 0))] * 2,
        compiler_params=_params("parallel"),
        name="rope_tables",
    )(pos_rep, inv_row, jnp.asarray(spread, BF16), jnp.asarray(base))
    return cos_full.reshape(b * s, LANES), sin_full.reshape(b * s, LANES)


def _mla_prep_body(qa_ref, kvc_ref, kpe_ref, cos_ref, sin_ref, qan_ref, kvn_ref, wq_ref, wk_ref,
                   wv_ref, qn_ref, kn_ref, q_ref, k_ref, v_ref):
    hw = MLA_HEADS * LANES
    cosf = cos_ref[...]
    sinf = sin_ref[...]
    qscale = MLA_QK ** -0.5 * math.log2(math.e)
    q_cos = (qn_ref[:, :LANES] * qscale) * cosf
    q_sin = (qn_ref[:, LANES:] * qscale) * sinf
    k_cos = kn_ref[:, :LANES] * cosf
    kpe = kpe_ref[:, :LANES]
    k_rot = kpe_ref[:, LANES:] * (kn_ref[:, LANES:] * sinf)

    def inv_rms(t):
        return lax.rsqrt(jnp.sum(t * t, axis=-1, keepdims=True) * (1.0 / MLA_QK) + EPS)

    qa = _rms(qa_ref[...].astype(F32), qan_ref[...]).astype(BF16)
    q_all = _dot(qa, wq_ref[...])
    kvc = _rms(kvc_ref[...].astype(F32), kvn_ref[...]).astype(BF16)
    k_all = _dot(kvc, wk_ref[...])
    v_ref[...] = _dot(kvc, wv_ref[...]).astype(v_ref.dtype)
    for hd in range(MLA_HEADS):
        sl = slice(hd * LANES, (hd + 1) * LANES)
        yq = q_all[:, sl]
        yq_rot = q_all[:, hw + hd * LANES:hw + (hd + 1) * LANES]
        q_ref[:, sl] = ((yq * q_cos + yq_rot * q_sin) * inv_rms(yq)).astype(q_ref.dtype)
        yk = k_all[:, sl] + kpe
        k_ref[:, sl] = ((yk * k_cos + k_rot) * inv_rms(yk)).astype(k_ref.dtype)


def _mla_prep(qa, kvc, kpe, cos_full, sin_full, qan, kvn, wq, wk, wv, qn, kn):
    t = qa.shape[0]
    tm = min(TOKEN_TILE, t)
    row = lambda n: pl.BlockSpec((tm, n), lambda i: (i, 0))
    hw = MLA_HEADS * LANES
    return pl.pallas_call(
        _mla_prep_body,
        out_shape=[jax.ShapeDtypeStruct((t, hw), BF16), jax.ShapeDtypeStruct((t, hw), BF16),
                   jax.ShapeDtypeStruct((t, MLA_HEADS * MLA_V), BF16)],
        grid=(t // tm,),
        in_specs=[row(MLA_Q_RANK), row(MLA_KV_RANK), row(2 * LANES), row(LANES), row(LANES),
                  _resident(qan.shape), _resident(kvn.shape), _resident(wq.shape),
                  _resident(wk.shape), _resident(wv.shape), _resident(qn.shape), _resident(kn.shape)],
        out_specs=[row(hw), row(hw), row(MLA_HEADS * MLA_V)],
        compiler_params=_params("parallel"),
        name="mla_prep",
    )(qa, kvc, kpe, cos_full, sin_full, qan, kvn, wq, wk, wv, qn, kn)


def _mla_attn_body(q_ref, k_ref, v_ref, o_ref, *, blk, n_blk):
    row = lax.broadcasted_iota(jnp.int32, (blk, blk), 0)
    col = lax.broadcasted_iota(jnp.int32, (blk, blk), 1)
    causal = col <= row
    lane = lax.broadcasted_iota(jnp.int32, (blk, LANES), 1)
    for i in range(n_blk):
        rows = slice(i * blk, (i + 1) * blk)
        past = slice(0, i * blk)
        outs = []
        for hd in range(2):
            sl = slice(hd * LANES, (hd + 1) * LANES)
            q = q_ref[0, rows, sl]
            s_d = jnp.where(causal, _dot_nt(q, k_ref[0, rows, sl]), -jnp.inf)
            m = jnp.max(s_d, axis=-1, keepdims=True)
            if i:
                s_p = _dot_nt(q, k_ref[0, past, sl])
                m = jnp.maximum(m, jnp.max(s_p, axis=-1, keepdims=True))
            p_d = jnp.exp2(s_d - m)
            den = jnp.sum(p_d, axis=-1, keepdims=True)
            acc = _dot(p_d.astype(BF16), v_ref[0, rows, :])
            if i:
                p_p = jnp.exp2(s_p - m)
                den = den + jnp.sum(p_p, axis=-1, keepdims=True)
                acc = acc + _dot(p_p.astype(BF16), v_ref[0, past, :])
            outs.append(acc / den)
        o_ref[0, rows, :] = jnp.where(lane < MLA_V, outs[0], outs[1]).astype(o_ref.dtype)


def _mla_attention(q, k, v):
    b, s, _ = q.shape
    blk = min(MLA_BLOCK, s)
    pairs = MLA_HEADS // 2
    return pl.pallas_call(
        functools.partial(_mla_attn_body, blk=blk, n_blk=s // blk),
        out_shape=jax.ShapeDtypeStruct((b, s, MLA_HEADS * MLA_V), BF16),
        grid=(b, pairs),
        in_specs=[
            pl.BlockSpec((1, s, 2 * LANES), lambda bb, p: (bb, 0, p)),
            pl.BlockSpec((1, s, 2 * LANES), lambda bb, p: (bb, 0, p)),
            pl.BlockSpec((1, s, LANES), lambda bb, p: (bb, 0, p)),
        ],
        out_specs=pl.BlockSpec((1, s, LANES), lambda bb, p: (bb, 0, p)),
        compiler_params=_params("parallel", "parallel"),
        name="mla_causal_attention",
    )(q, k, v)


def _softplus(x):
    return jnp.maximum(x, 0.0) + jnp.log(1.0 + jnp.exp(-jnp.abs(x)))


def _ssd_body(xbc_ref, z_ref, dt_ref, cw_ref, cb_ref, dtb_ref, alog_ref, dsk_ref, nrm_ref,
              e64_ref, shift_ref, y_ref, xpad_ref, st_ref, *, n_chunks):
    L = SSM_CHUNK
    pad = SSD_HISTORY

    @pl.when(pl.program_id(1) == 0)
    def _():
        xpad_ref[...] = jnp.zeros_like(xpad_ref)
        st_ref[...] = jnp.zeros_like(st_ref)

    xpad_ref[pad:pad + n_chunks * L, :] = xbc_ref[0]
    convs = []
    for ci in range(n_chunks):
        taps = _dot(shift_ref[...], xpad_ref[ci * L:ci * L + SSD_WINDOW, :])
        conv = cb_ref[...]
        for back in range(SSM_CONV):
            tap = SSM_CONV - 1 - back
            conv = conv + cw_ref[tap:tap + 1, :] * taps[back * L:(back + 1) * L]
        convs.append(conv)
    xpad_ref[0:pad, :] = xpad_ref[n_chunks * L:n_chunks * L + pad, :]
    for ci in range(n_chunks):
        _ssd_chunk(convs[ci], slice(ci * L, (ci + 1) * L), z_ref, dt_ref, dtb_ref, alog_ref, dsk_ref,
                   nrm_ref, e64_ref, y_ref, st_ref)


def _ssd_chunk(conv, rows, z_ref, dt_ref, dtb_ref, alog_ref, dsk_ref, nrm_ref, e64_ref, y_ref, st_ref):
    L = SSM_CHUNK
    act = _silu(conv)
    xs = act[:, :SSM_D]
    gn = SSM_GROUPS * SSM_STATE
    b_all = act[:, SSM_D:SSM_D + gn]
    c_all = act[:, SSM_D + gn:]

    lane = lax.broadcasted_iota(jnp.int32, (L, LANES), 1)
    row = lax.broadcasted_iota(jnp.int32, (L, L), 0)
    col = lax.broadcasted_iota(jnp.int32, (L, L), 1)
    dt = jnp.where(lane < SSM_HEADS, _softplus(dt_ref[0, rows] + dtb_ref[...]), 0.0)
    a = dt * (-math.log2(math.e) * jnp.exp(alog_ref[...]))
    a_parts = _split_bf16(a, 3)
    tri = jnp.where(col <= row, 1.0, 0.0).astype(BF16)
    cs3 = _dot(tri, jnp.concatenate(a_parts, axis=1))
    cs = cs3[:, :LANES] + cs3[:, LANES:2 * LANES] + cs3[:, 2 * LANES:]
    cs_t = cs.T
    cs_last = cs[L - 1:L, :]
    ecs = jnp.exp2(cs)
    stacked = jnp.concatenate([dt, ecs, jnp.exp2(cs_last - cs)], axis=0)
    wide = _dot(jnp.concatenate(_split_bf16(stacked, 2), axis=1), e64_ref[...])
    dt_w, ecs_w, dec_w = wide[:L], wide[L:2 * L], wide[2 * L:]

    xd = xs * dt_w
    xd_b = xd.astype(BF16)
    xdw_b = (xd * dec_w).astype(BF16)
    chunk_decay = ecs_w[L - 1:L, :]

    heads_per_group = SSM_HEADS // SSM_GROUPS
    gw = heads_per_group * SSM_HEAD_DIM
    cs_col = jnp.concatenate([jnp.broadcast_to(cs[:, h:h + 1], (L, L)) for h in range(SSM_HEADS)], axis=1)
    cs_row = jnp.concatenate([jnp.broadcast_to(cs_t[h:h + 1, :], (L, L)) for h in range(SSM_HEADS)], axis=1)
    decay = jnp.exp2(jnp.minimum(cs_col - cs_row, 0.0))
    b_bf = b_all.astype(BF16)
    c_bf = c_all.astype(BF16)
    cbs = [_dot_nt(c_bf[:, g * SSM_STATE:(g + 1) * SSM_STATE], b_bf[:, g * SSM_STATE:(g + 1) * SSM_STATE])
           for g in range(SSM_GROUPS)]
    cbs = [jnp.where(col <= row, cb, 0.0) for cb in cbs]
    cb_all = jnp.concatenate([cbs[h // heads_per_group] for h in range(SSM_HEADS)], axis=1)
    mix = (cb_all * decay).astype(BF16)

    y_parts = []
    for pi in range(SSM_HEADS // 2):
        pair = xd_b[:, pi * LANES:(pi + 1) * LANES]
        zero = jnp.zeros_like(pair)
        rhs = jnp.concatenate([jnp.where(lane < SSM_HEAD_DIM, pair, zero),
                               jnp.where(lane < SSM_HEAD_DIM, zero, pair)], axis=0)
        y_parts.append(_dot(mix[:, 2 * pi * L:(2 * pi + 2) * L], rhs))
    yoff_parts = []
    for g in range(SSM_GROUPS):
        st_g = st_ref[:, g * gw:(g + 1) * gw]
        yoff_parts.append(_dot(c_bf[:, g * SSM_STATE:(g + 1) * SSM_STATE], st_g.astype(BF16)))
        bg_t = b_all[:, g * SSM_STATE:(g + 1) * SSM_STATE].T.astype(BF16)
        upd = _dot(bg_t, xdw_b[:, g * gw:(g + 1) * gw])
        st_ref[:, g * gw:(g + 1) * gw] = st_g * chunk_decay[:, g * gw:(g + 1) * gw] + upd

    y = jnp.concatenate(y_parts, axis=1) + ecs_w * jnp.concatenate(yoff_parts, axis=1)
    y = (y + dsk_ref[...] * xs) * _silu(z_ref[0, rows].astype(F32))
    for g in range(SSM_GROUPS):
        sl = slice(g * gw, (g + 1) * gw)
        y_ref[0, rows, sl] = _rms(y[:, sl], nrm_ref[:, sl]).astype(y_ref.dtype)


def _ssd(xbc, z, dt, conv_w, conv_b, dt_bias, a_log, d_skip, ssm_norm):
    b, s, _ = xbc.shape
    L = SSM_CHUNK
    heads = np.arange(2 * LANES)[:, None] % LANES
    e64 = jnp.asarray(heads == (np.arange(SSM_D)[None, :] // SSM_HEAD_DIM), BF16)
    t_idx = np.arange(SSM_CONV * L)
    shift = np.zeros((SSM_CONV * L, SSD_WINDOW), np.float32)
    shift[t_idx, SSD_HISTORY + t_idx % L - t_idx // L] = 1.0
    shift = jnp.asarray(shift, BF16)
    padh = lambda v: jnp.pad(v.reshape(1, SSM_HEADS), ((0, 0), (0, LANES - SSM_HEADS)))
    dsk = jnp.repeat(d_skip, SSM_HEAD_DIM).reshape(1, SSM_D)
    nc = SSD_CHUNKS_PER_STEP
    blk = lambda n: pl.BlockSpec((1, nc * L, n), lambda bb, c: (bb, c, 0))
    return pl.pallas_call(
        functools.partial(_ssd_body, n_chunks=nc),
        out_shape=jax.ShapeDtypeStruct((b, s, SSM_D), BF16),
        grid=(b, s // (nc * L)),
        in_specs=[blk(SSM_XBC), blk(SSM_D), blk(LANES),
                  _resident((SSM_CONV, SSM_XBC)), _resident((1, SSM_XBC)), _resident((1, LANES)),
                  _resident((1, LANES)), _resident((1, SSM_D)), _resident((1, SSM_D)),
                  _resident(e64.shape), _resident(shift.shape)],
        out_specs=blk(SSM_D),
        scratch_shapes=[pltpu.VMEM(((nc - 1) * L + SSD_WINDOW, SSM_XBC), BF16), pltpu.VMEM((SSM_STATE, SSM_D), F32)],
        compiler_params=_params("parallel", "arbitrary"),
        name="ssd_scan",
    )(xbc, z, dt, conv_w, conv_b.reshape(1, SSM_XBC), padh(dt_bias), padh(a_log), dsk,
      ssm_norm.reshape(1, SSM_D), e64, shift)


def _swa_body(q_ref, k_ref, v_ref, pc_ref, pr_ref, sink_ref, o_ref, *, tq):
    W = SWA_WINDOW
    grp = SWA_Q_HEADS // SWA_KV_HEADS
    log2e = math.log2(math.e)
    slopes = [log2e * 2.0 ** (-8.0 * (h + 1) / SWA_Q_HEADS) for h in range(SWA_Q_HEADS)]
    i = pl.program_id(1)
    lane = lax.broadcasted_iota(jnp.int32, (W, LANES), 1)
    qi = lax.broadcasted_iota(jnp.int32, (W, 2 * W), 0)
    ki = lax.broadcasted_iota(jnp.int32, (W, 2 * W), 1)
    sinks = sink_ref[...] * log2e
    sink_wide = jnp.concatenate(
        [jnp.broadcast_to(sinks[0:1, h:h + 1], (W, 2 * W)) for h in range(SWA_Q_HEADS)], axis=0)
    krow = lax.broadcasted_iota(jnp.int32, (2 * W, LANES), 0)
    ones = jnp.ones((2 * W, LANES), BF16)
    for jb in range(tq // W):
        j = i * (tq // W) + jb
        kb0 = jnp.maximum(j - 1, 0)
        ks = pl.multiple_of(kb0 * W, W)
        rel = (j - kb0) * W + qi - ki
        valid = (rel >= 0) & (rel < W)
        pq = pc_ref[0, jb * W:(jb + 1) * W, :]
        pk = jnp.concatenate([pr_ref[0, kb0], pr_ref[0, kb0 + 1]], axis=1)
        dist = jnp.where(valid, jnp.abs(pq - pk).astype(F32), jnp.inf)
        bias = jnp.concatenate([slopes[h] * dist for h in range(SWA_Q_HEADS)], axis=0)
        rows = slice(jb * W, (jb + 1) * W)
        logits = []
        for kv in range(SWA_KV_HEADS):
            kk = k_ref[0, pl.ds(ks, 2 * W), kv * LANES:(kv + 1) * LANES]
            qs = jnp.concatenate(
                [q_ref[0, rows, (kv * grp + g) * LANES:(kv * grp + g + 1) * LANES] for g in range(grp)],
                axis=0)
            logits.append(_dot_nt(qs, kk))
        s = jnp.concatenate(logits, axis=0) - bias
        sink_at = jnp.where(j == 0, 2 * W - 1, 0)
        s = jnp.where(jnp.concatenate([ki == sink_at] * SWA_Q_HEADS, axis=0), sink_wide, s)
        m = jnp.max(s, axis=-1, keepdims=True)
        pb = jnp.exp2(s - m).astype(BF16)
        outs = []
        for kv in range(SWA_KV_HEADS):
            vv = v_ref[0, pl.ds(ks, 2 * W), kv * LANES:(kv + 1) * LANES]
            vv = jnp.where(krow == sink_at, jnp.zeros_like(vv), vv)
            outs.append(_dot(pb[kv * grp * W:(kv + 1) * grp * W], jnp.concatenate([vv, ones], axis=1)))
        o = jnp.concatenate(outs, axis=0)
        o = o[:, :LANES] / o[:, LANES:]
        for pr in range(SWA_Q_HEADS // 2):
            o_pair = jnp.where(lane < SWA_HD, o[2 * pr * W:(2 * pr + 1) * W], o[(2 * pr + 1) * W:(2 * pr + 2) * W])
            o_ref[0, rows, pr * LANES:(pr + 1) * LANES] = o_pair.astype(o_ref.dtype)


def _swa(q, k, v, pos, sinks):
    b, s, _ = q.shape
    tq = min(SWA_TILE, s)
    W = SWA_WINDOW
    nb = s // W
    pos_col = pos.reshape(b, s, 1)
    pos_row = jnp.concatenate([pos.reshape(b, nb, 1, W), jnp.zeros((b, 1, 1, W), pos.dtype)], axis=1)
    return pl.pallas_call(
        functools.partial(_swa_body, tq=tq),
        out_shape=jax.ShapeDtypeStruct((b, s, SWA_Q_HEADS * SWA_HD), BF16),
        grid=(b, s // tq),
        in_specs=[
            pl.BlockSpec((1, tq, SWA_Q_HEADS * LANES), lambda bb, i: (bb, i, 0)),
            pl.BlockSpec((1, s, SWA_KV_HEADS * LANES), lambda bb, i: (bb, 0, 0)),
            pl.BlockSpec((1, s, SWA_KV_HEADS * LANES), lambda bb, i: (bb, 0, 0)),
            pl.BlockSpec((1, tq, 1), lambda bb, i: (bb, i, 0)),
            pl.BlockSpec((1, nb + 1, 1, W), lambda bb, i: (bb, 0, 0, 0)),
            _resident((1, SWA_Q_HEADS)),
        ],
        out_specs=pl.BlockSpec((1, tq, SWA_Q_HEADS * SWA_HD), lambda bb, i: (bb, i, 0)),
        compiler_params=_params("parallel", "parallel"),
        name="swa_sink_attention",
    )(q, k, v, pos_col, pos_row, sinks.reshape(1, SWA_Q_HEADS))


def _log_sigmoid(x):
    return jnp.minimum(x, 0.0) - jnp.log(1.0 + jnp.exp(-jnp.abs(x)))


def _gla_body(q_ref, k_ref, v_ref, ga_ref, gr_ref, wgb_ref, gb_ref, nrm_ref, o_ref, st_ref, *, tg):
    C = GLA_CHUNK

    @pl.when(pl.program_id(1) == 0)
    def _():
        st_ref[...] = jnp.zeros_like(st_ref)

    row = lax.broadcasted_iota(jnp.int32, (tg, tg), 0)
    col = lax.broadcasted_iota(jnp.int32, (tg, tg), 1)
    shift = C.bit_length() - 1
    same_chunk = lax.shift_right_logical(row, shift) == lax.shift_right_logical(col, shift)
    intra = same_chunk & (col <= row)
    hw = GLA_HEADS * LANES
    g = _log_sigmoid(_dot(ga_ref[0], wgb_ref[...]) + gb_ref[...]) * (math.log2(math.e) / GLA_TAU)
    masks = jnp.concatenate([jnp.where(intra, 1.0, 0.0), jnp.where(same_chunk, 1.0, 0.0)], axis=0).astype(BF16)
    sums = _dot(masks, jnp.concatenate(_split_bf16(g, 2), axis=1))
    sums = sums[:, :hw] + sums[:, hw:]
    bcum, btot = sums[:tg], sums[tg:]
    q_dec = (q_ref[0].astype(F32) * jnp.exp2(bcum)).astype(BF16)
    kf = k_ref[0].astype(F32)
    k_inv = (kf * jnp.exp2(-bcum)).astype(BF16)
    k_end = (kf * jnp.exp2(btot - bcum)).astype(BF16)
    chunk_dec = jnp.exp2(btot)
    n_c = tg // C
    heads = [slice(hd * LANES, (hd + 1) * LANES) for hd in range(GLA_HEADS)]
    chunks = [slice(c * C, (c + 1) * C) for c in range(n_c)]
    vs = [v_ref[0, :, sl] for sl in heads]
    o_intra = [_dot(jnp.where(intra, _dot_nt(q_dec[:, sl], k_inv[:, sl]), 0.0).astype(BF16), vh)
               for sl, vh in zip(heads, vs)]
    kvs = [[_dot_tn(vh[rows], k_end[rows, sl]) for rows in chunks] for sl, vh in zip(heads, vs)]
    for hd, sl in enumerate(heads):
        st = st_ref[hd]
        inter = []
        for c, rows in enumerate(chunks):
            inter.append(_dot_nt(q_dec[rows, sl], st.astype(BF16)))
            st = st * chunk_dec[c * C:c * C + 1, sl] + kvs[hd][c]
        st_ref[hd] = st
        o = o_intra[hd] + jnp.concatenate(inter, axis=0)
        o = _rms(o, nrm_ref[...]) * _silu(gr_ref[0, :, sl].astype(F32))
        o_ref[0, :, sl] = o.astype(o_ref.dtype)


def _gla(q, k, v, ga, gr, wgb, gate_bias, gla_norm):
    b, s, _ = q.shape
    tg = min(GLA_TILE, s)
    hw = GLA_HEADS * LANES
    blk = lambda n: pl.BlockSpec((1, tg, n), lambda bb, i: (bb, i, 0))
    return pl.pallas_call(
        functools.partial(_gla_body, tg=tg),
        out_shape=jax.ShapeDtypeStruct((b, s, GLA_HEADS * GLA_DV), BF16),
        grid=(b, s // tg),
        in_specs=[blk(hw), blk(hw), blk(GLA_HEADS * GLA_DV), blk(LANES), blk(GLA_HEADS * GLA_DV),
                  _resident(wgb.shape), _resident((1, hw)), _resident((1, GLA_DV))],
        out_specs=blk(GLA_HEADS * GLA_DV),
        scratch_shapes=[pltpu.VMEM((GLA_HEADS, GLA_DV, LANES), F32)],
        compiler_params=_params("parallel", "arbitrary"),
        name="gla_chunked",
    )(q, k, v, ga, gr, wgb, gate_bias, gla_norm.reshape(1, GLA_DV))


def _pad_heads(w, n_heads, width, lanes=LANES, offset=0):
    lead = w.shape[:-1]
    w = w.reshape(lead + (n_heads, width))
    cfg = [(0, 0)] * len(lead) + [(0, 0), (offset, lanes - width - offset)]
    return jnp.pad(w, cfg).reshape(lead + (n_heads * lanes,))


def _rot_cols(w):
    half = MLA_ROPE // 2
    lane = np.arange(LANES)
    src = np.where((lane >= MLA_NOPE) & (lane < MLA_NOPE + half), lane + half,
                   np.where((lane >= MLA_NOPE + half) & (lane < MLA_QK), lane - half, 0))
    keep = jnp.asarray((lane >= MLA_NOPE) & (lane < MLA_QK), w.dtype)
    blocks = w.reshape(w.shape[:-1] + (w.shape[-1] // LANES, LANES))
    return (blocks[..., src] * keep).reshape(w.shape)


def _pad_cols(w, total, offset=0):
    return jnp.pad(w, [(0, 0)] * (w.ndim - 1) + [(offset, total - w.shape[-1] - offset)])


def _even_mixer(x2, b, s, pos, mix_norm, w_in, conv_w, conv_b, dt_bias, a_log, d_skip, ssm_norm,
                q_a_norm, w_q_b, kv_a_norm, w_kv_b, q_norm, k_norm, w_out):
    t = b * s
    o0 = 0
    cols = {}
    for name, n in (("z", SSM_D), ("xbc", SSM_XBC), ("dt", SSM_HEADS), ("qa", MLA_Q_RANK),
                    ("kvc", MLA_KV_RANK), ("kpe", MLA_ROPE)):
        cols[name] = w_in[:, o0:o0 + n]
        o0 += n
    w_kpe = _pad_cols(cols["kpe"], LANES, offset=MLA_NOPE)
    weights = [cols["z"], cols["xbc"], _pad_cols(cols["dt"], LANES), cols["qa"], cols["kvc"],
               jnp.concatenate([w_kpe, _rot_cols(w_kpe)], axis=1)]
    weights = [w.astype(BF16) for w in weights]
    z, xbc, dt, qa, kvc, kpe = _inproj(x2, mix_norm, weights, [BF16, BF16, F32, BF16, BF16, F32],
                                       [0] * 6, [])
    y = _ssd(xbc.reshape(b, s, -1), z.reshape(b, s, -1), dt.reshape(b, s, -1), conv_w, conv_b,
             dt_bias, a_log, d_skip, ssm_norm).reshape(t, SSM_D)

    cos_full, sin_full = _rope_tables(pos)
    wq = _pad_heads(w_q_b, MLA_HEADS, MLA_QK)
    wq = jnp.concatenate([wq, _rot_cols(wq)], axis=1).astype(BF16)
    with_rot = lambda g: jnp.concatenate([g, _rot_cols(g)], axis=1)
    w_kv = w_kv_b.reshape(MLA_KV_RANK, MLA_HEADS, MLA_NOPE + MLA_V)
    wk = _pad_heads(w_kv[:, :, :MLA_NOPE].reshape(MLA_KV_RANK, -1), MLA_HEADS, MLA_NOPE).astype(BF16)
    wv = w_kv[:, :, MLA_NOPE:].reshape(MLA_KV_RANK, MLA_HEADS * MLA_V).astype(BF16)
    q, k, v = _mla_prep(qa, kvc, kpe, cos_full, sin_full, q_a_norm.reshape(1, -1),
                        kv_a_norm.reshape(1, -1), wq, wk, wv,
                        with_rot(_pad_cols(q_norm.reshape(1, -1), LANES)),
                        with_rot(_pad_cols(k_norm.reshape(1, -1), LANES)))
    o = _mla_attention(q.reshape(b, s, -1), k.reshape(b, s, -1), v.reshape(b, s, -1)).reshape(t, -1)
    return [y, o], [w_out[:SSM_D], w_out[SSM_D:]]


def _odd_mixer(x2, b, s, pos, mix_norm, w_in, q_norm, k_norm, sinks, w_gate_b, gate_bias, gla_norm,
               w_out):
    t = b * s
    sizes = [SWA_Q_HEADS * SWA_HD, SWA_KV_HEADS * SWA_HD, SWA_KV_HEADS * SWA_HD,
             GLA_HEADS * GLA_DK, GLA_HEADS * GLA_DK, GLA_HEADS * GLA_DV, GLA_RANK, GLA_HEADS * GLA_DV]
    parts, o0 = [], 0
    for n in sizes:
        parts.append(w_in[:, o0:o0 + n])
        o0 += n
    wq, wk, wv, wgq, wgk, wgv, wga, wgr = parts
    v_dup = jnp.repeat(wv.reshape(D_MODEL, SWA_KV_HEADS, 1, SWA_HD), 2, axis=2).reshape(D_MODEL, -1)
    weights = [_pad_heads(wq, SWA_Q_HEADS, SWA_HD), _pad_heads(wk, SWA_KV_HEADS, SWA_HD), v_dup,
               _pad_heads(wgq * (GLA_DK ** -0.5), GLA_HEADS, GLA_DK), _pad_heads(wgk, GLA_HEADS, GLA_DK),
               wgv, _pad_cols(wga, LANES), wgr]
    weights = [w.astype(BF16) for w in weights]
    gains = [_pad_cols(q_norm.reshape(1, -1) * (SWA_HD ** -0.5 * math.log2(math.e)), LANES),
             _pad_cols(k_norm.reshape(1, -1), LANES)]
    q, k, v, gq, gk, gv, ga, gr = _inproj(x2, mix_norm, weights, [BF16] * 8,
                                          [SWA_HD, SWA_HD, 0, 0, 0, 0, 0, 0], gains)
    o_swa = _swa(q.reshape(b, s, -1), k.reshape(b, s, -1), v.reshape(b, s, -1), pos, sinks)
    wgb = _pad_heads(jnp.pad(w_gate_b, ((0, LANES - GLA_RANK), (0, 0))), GLA_HEADS, GLA_DK).astype(BF16)
    gb = _pad_heads(gate_bias.reshape(1, -1), GLA_HEADS, GLA_DK)
    o_gla = _gla(gq.reshape(b, s, -1), gk.reshape(b, s, -1), gv.reshape(b, s, -1), ga.reshape(b, s, -1),
                 gr.reshape(b, s, -1), wgb, gb, gla_norm)
    n_swa = SWA_Q_HEADS * SWA_HD
    return [o_swa.reshape(t, -1), o_gla.reshape(t, -1)], [w_out[:n_swa], w_out[n_swa:]]


def kernel(x, positions, pre_norm, pre_w_gate, pre_w_up, pre_w_down, mix_norm, post_norm, post_w_gate,
           post_w_up, post_w_down, e_w_in, e_conv_w, e_conv_b, e_dt_bias, e_a_log, e_d_skip, e_ssm_norm,
           e_q_a_norm, e_w_q_b, e_kv_a_norm, e_w_kv_b, e_q_norm, e_k_norm, e_w_out, o_w_in, o_q_norm,
           o_k_norm, o_sinks, o_w_gate_b, o_gate_bias, o_gla_norm, o_w_out):
    b, s, d = x.shape
    depth = pre_norm.shape[0]
    x2 = x.reshape(b * s, d)
    for layer in range(depth):
        x2 = _ffn(x2, pre_norm[layer], pre_w_gate[layer], pre_w_up[layer], pre_w_down[layer])
        j = layer // 2
        if layer % 2 == 0:
            acts, w_outs = _even_mixer(x2, b, s, positions, mix_norm[layer], e_w_in[j], e_conv_w[j],
                                       e_conv_b[j], e_dt_bias[j], e_a_log[j], e_d_skip[j], e_ssm_norm[j],
                                       e_q_a_norm[j], e_w_q_b[j], e_kv_a_norm[j], e_w_kv_b[j], e_q_norm[j],
                                       e_k_norm[j], e_w_out[j])
        else:
            acts, w_outs = _odd_mixer(x2, b, s, positions, mix_norm[layer], o_w_in[j], o_q_norm[j],
                                      o_k_norm[j], o_sinks[j], o_w_gate_b[j], o_gate_bias[j],
                                      o_gla_norm[j], o_w_out[j])
        x2 = _ffn(x2, post_norm[layer], post_w_gate[layer], post_w_up[layer], post_w_down[layer],
                  acts, w_outs)
    return x2.reshape(b, s, d)
```

```python
import functools
import math

import jax
import jax.numpy as jnp
import numpy as np
from jax import lax
from jax.experimental import pallas as pl
from jax.experimental.pallas import tpu as pltpu

F32 = jnp.float32
BF16 = jnp.bfloat16
HIGHEST = lax.Precision.HIGHEST

D_MODEL = 1024
D_FF = 2816
FFN_RES = 0.5
EPS = 1e-6
SSM_HEADS = 16
SSM_HEAD_DIM = 64
SSM_D = SSM_HEADS * SSM_HEAD_DIM
SSM_GROUPS = 4
SSM_STATE = 128
SSM_CONV = 4
SSM_CHUNK = 128
SSM_XBC = SSM_D + 2 * SSM_GROUPS * SSM_STATE
MLA_HEADS = 8
MLA_Q_RANK = 384
MLA_KV_RANK = 256
MLA_NOPE = 64
MLA_ROPE = 32
MLA_QK = MLA_NOPE + MLA_ROPE
MLA_V = 64
ROPE_THETA = 10000.0
SWA_Q_HEADS = 8
SWA_KV_HEADS = 2
SWA_HD = 64
SWA_WINDOW = 128
GLA_HEADS = 4
GLA_DK = 64
GLA_DV = 128
GLA_RANK = 16
GLA_TAU = 16.0
GLA_CHUNK = 64

LANES = 128
VMEM_LIMIT_BYTES = 56 * 1024 * 1024

FF_CHUNK = 256
TOKEN_TILE = 1024
FFN_TILE = 1024
MLA_BLOCK = 512
SWA_TILE = 512
GLA_TILE = 256
SSD_HISTORY = 16
SSD_WINDOW = 256
SSD_CHUNKS_PER_STEP = 2


def _params(*sem):
    return pltpu.CompilerParams(dimension_semantics=sem, vmem_limit_bytes=VMEM_LIMIT_BYTES)


def _resident(shape):
    nd = len(shape)
    return pl.BlockSpec(shape, lambda *_: (0,) * nd, pipeline_mode=pl.Buffered(1))


def _rms(x, gain):
    return x * lax.rsqrt(jnp.mean(x * x, axis=-1, keepdims=True) + EPS) * gain


def _silu(x):
    return x * jax.nn.sigmoid(x)


def _dot(a, b, **kw):
    return jnp.dot(a, b, preferred_element_type=F32, **kw)


def _dot_nt(a, b):
    return lax.dot_general(a, b, (((1,), (1,)), ((), ())), preferred_element_type=F32)


def _split_bf16(x, n):
    parts = []
    for _ in range(n - 1):
        p = x.astype(BF16)
        parts.append(p)
        x = x - p.astype(F32)
    parts.append(x.astype(BF16))
    return parts


def _dot_tn(a, b):
    return lax.dot_general(a, b, (((0,), (0,)), ((), ())), preferred_element_type=F32)


def _ffn_body(x_ref, g_ref, wg_ref, wu_ref, wd_ref, *refs, n_mix):
    mix_a, mix_w = refs[:n_mix], refs[n_mix:2 * n_mix]
    o_ref, h_ref, acc_ref = refs[2 * n_mix:]
    x = x_ref[...]
    if n_mix:
        mix = _dot(mix_a[0][...], mix_w[0][...])
        for a_ref, w_ref in zip(mix_a[1:], mix_w[1:]):
            mix = mix + _dot(a_ref[...], w_ref[...])
        x = x + mix
    acc_ref[...] = x
    h_ref[...] = _rms(acc_ref[...], g_ref[...]).astype(BF16)
    for c in range(D_FF // FF_CHUNK):
        cols = slice(c * FF_CHUNK, (c + 1) * FF_CHUNK)
        gate = _dot(h_ref[...], wg_ref[:, cols])
        up = _dot(h_ref[...], wu_ref[:, cols])
        acc_ref[...] += _dot((_silu(gate) * up).astype(BF16), wd_ref[cols, :])
    o_ref[...] = acc_ref[...]


def _ffn(x2, norm, w_gate, w_up, w_down, mix_acts=(), mix_weights=()):
    t = x2.shape[0]
    tm = min(FFN_TILE, t)
    weights = [w_gate.astype(BF16), w_up.astype(BF16), (FFN_RES * w_down).astype(BF16)]
    mix_weights = [w.astype(BF16) for w in mix_weights]
    row = lambda n: pl.BlockSpec((tm, n), lambda i: (i, 0))
    return pl.pallas_call(
        functools.partial(_ffn_body, n_mix=len(mix_acts)),
        out_shape=jax.ShapeDtypeStruct((t, D_MODEL), F32),
        grid=(t // tm,),
        in_specs=[row(D_MODEL), _resident((1, D_MODEL))] + [_resident(w.shape) for w in weights]
        + [row(a.shape[1]) for a in mix_acts] + [_resident(w.shape) for w in mix_weights],
        out_specs=row(D_MODEL),
        scratch_shapes=[pltpu.VMEM((tm, D_MODEL), BF16), pltpu.VMEM((tm, D_MODEL), F32)],
        compiler_params=_params("parallel"),
        name="swiglu_half_step",
    )(x2, norm.reshape(1, D_MODEL), *weights, *mix_acts, *mix_weights)


def _head_norm(t, gain, n_valid):
    ms = jnp.sum(t * t, axis=-1, keepdims=True) * (1.0 / n_valid)
    return t * lax.rsqrt(ms + EPS) * gain


def _inproj_body(x_ref, g_ref, *refs, head_norms):
    n_out = len(head_norms)
    w_refs = refs[:n_out]
    n_gain = sum(1 for hn in head_norms if hn)
    gain_refs = refs[n_out:n_out + n_gain]
    o_refs = refs[n_out + n_gain:]
    h = _rms(x_ref[...], g_ref[...]).astype(BF16)
    gi = 0
    for w_ref, o_ref, hn in zip(w_refs, o_refs, head_norms):
        y = _dot(h, w_ref[...])
        if hn:
            gain = gain_refs[gi][...]
            gi += 1
            for hd in range(y.shape[1] // LANES):
                sl = slice(hd * LANES, (hd + 1) * LANES)
                o_ref[:, sl] = _head_norm(y[:, sl], gain, hn).astype(o_ref.dtype)
        else:
            o_ref[...] = y.astype(o_ref.dtype)


def _inproj(x2, norm, weights, out_dtypes, head_norms, gains):
    t = x2.shape[0]
    tm = min(TOKEN_TILE, t)
    in_specs = [pl.BlockSpec((tm, D_MODEL), lambda i: (i, 0)), _resident((1, D_MODEL))]
    in_specs += [_resident(w.shape) for w in weights]
    in_specs += [_resident(g.shape) for g in gains]
    return pl.pallas_call(
        functools.partial(_inproj_body, head_norms=tuple(head_norms)),
        out_shape=[jax.ShapeDtypeStruct((t, w.shape[1]), dt) for w, dt in zip(weights, out_dtypes)],
        grid=(t // tm,),
        in_specs=in_specs,
        out_specs=[pl.BlockSpec((tm, w.shape[1]), lambda i: (i, 0)) for w in weights],
        compiler_params=_params("parallel"),
        name="mixer_in_proj",
    )(x2, norm.reshape(1, D_MODEL), *weights, *gains)


def _rope_body(pos_ref, inv_ref, spread_ref, base_ref, cos_ref, sin_ref):
    ang = pos_ref[...] * inv_ref[...]
    terms = _split_bf16(jnp.cos(ang), 3) + _split_bf16(jnp.sin(ang), 3)
    lhs = jnp.concatenate(terms, axis=1)
    per_row = LANES // (MLA_ROPE // 2)
    tr = lhs.shape[0]
    for p in range(per_row):
        wide = _dot(lhs, spread_ref[p])
        cos_ref[pl.ds(p, tr, stride=per_row), :] = wide[:, :LANES] + base_ref[...]
        sin_ref[pl.ds(p, tr, stride=per_row), :] = wide[:, LANES:]


def _rope_tables(positions):
    b, s = positions.shape
    half = MLA_ROPE // 2
    inv = ROPE_THETA ** (-jnp.arange(0, MLA_ROPE, 2, dtype=F32) / MLA_ROPE)
    per_row = LANES // half
    rows = b * s // per_row
    pos_rep = jnp.repeat(positions.astype(F32).reshape(rows, per_row), half, axis=1)
    inv_row = jnp.tile(inv, per_row).reshape(1, LANES)
    n_terms = 3
    spread = np.zeros((per_row, 2 * n_terms * LANES, 2 * LANES), np.float32)
    f = np.arange(half)
    for p in range(per_row):
        for part in range(n_terms):
            src_c = part * LANES + p * half + f
            src_s = (n_terms + part) * LANES + p * half + f
            spread[p, src_c, MLA_NOPE + f] = 1.0
            spread[p, src_c, MLA_NOPE + half + f] = 1.0
            spread[p, src_s, LANES + MLA_NOPE + f] = -1.0
            spread[p, src_s, LANES + MLA_NOPE + half + f] = 1.0
    lane = np.arange(LANES)
    base = np.where((lane >= MLA_NOPE) & (lane < MLA_QK), 0.0, 1.0).astype(np.float32).reshape(1, LANES)
    tr = min(1024, rows)
    return pl.pallas_call(
        _rope_body,
        out_shape=[jax.ShapeDtypeStruct((b * s, LANES), F32)] * 2,
        grid=(rows // tr,),
        in_specs=[pl.BlockSpec((tr, LANES), lambda i: (i, 0)), _resident((1, LANES)),
                  _resident(spread.shape), _resident((1, LANES))],
        out_specs=[pl.BlockSpec((tr * per_row, LANES), lambda i: (i, 0))] * 2,
        compiler_params=_params("parallel"),
        name="rope_tables",
    )(pos_rep, inv_row, jnp.asarray(spread, BF16), jnp.asarray(base))


def _even_inproj_body(x_ref, g_ref, wz_ref, wxbc_ref, wdt_ref, wqa_ref, wkvc_ref, wkpe_ref, cos_ref, sin_ref,
                      qan_ref, kvn_ref, wq_ref, wk_ref, wv_ref, qn_ref, kn_ref,
                      z_ref, xbc_ref, dt_ref, q_ref, k_ref, v_ref):
    h = _rms(x_ref[...], g_ref[...]).astype(BF16)
    z_ref[...] = _dot(h, wz_ref[...]).astype(z_ref.dtype)
    xbc_ref[...] = _dot(h, wxbc_ref[...]).astype(xbc_ref.dtype)
    dt_ref[...] = _dot(h, wdt_ref[...])
    qa = _rms(_dot(h, wqa_ref[...]), qan_ref[...]).astype(BF16)
    kvc = _rms(_dot(h, wkvc_ref[...]), kvn_ref[...]).astype(BF16)
    kpe2 = _dot(h, wkpe_ref[...])

    hw = MLA_HEADS * LANES
    cosf = cos_ref[...]
    sinf = sin_ref[...]
    qscale = MLA_QK ** -0.5 * math.log2(math.e)
    q_cos = (qn_ref[:, :LANES] * qscale) * cosf
    q_sin = (qn_ref[:, LANES:] * qscale) * sinf
    k_cos = kn_ref[:, :LANES] * cosf
    kpe = kpe2[:, :LANES]
    k_rot = kpe2[:, LANES:] * (kn_ref[:, LANES:] * sinf)

    def inv_rms(t):
        return lax.rsqrt(jnp.sum(t * t, axis=-1, keepdims=True) * (1.0 / MLA_QK) + EPS)

    q_all = _dot(qa, wq_ref[...])
    k_all = _dot(kvc, wk_ref[...])
    v_ref[...] = _dot(kvc, wv_ref[...]).astype(v_ref.dtype)
    for hd in range(MLA_HEADS):
        sl = slice(hd * LANES, (hd + 1) * LANES)
        yq = q_all[:, sl]
        yq_rot = q_all[:, hw + hd * LANES:hw + (hd + 1) * LANES]
        q_ref[:, sl] = ((yq * q_cos + yq_rot * q_sin) * inv_rms(yq)).astype(q_ref.dtype)
        yk = k_all[:, sl] + kpe
        k_ref[:, sl] = ((yk * k_cos + k_rot) * inv_rms(yk)).astype(k_ref.dtype)


def _even_inproj(x2, norm, in_weights, cos_full, sin_full, qan, kvn, wq, wk, wv, qn, kn):
    t = x2.shape[0]
    tm = min(TOKEN_TILE, t)
    row = lambda n: pl.BlockSpec((tm, n), lambda i: (i, 0))
    hw = MLA_HEADS * LANES
    consts = [qan, kvn, wq, wk, wv, qn, kn]
    return pl.pallas_call(
        _even_inproj_body,
        out_shape=[jax.ShapeDtypeStruct((t, SSM_D), BF16), jax.ShapeDtypeStruct((t, SSM_XBC), BF16),
                   jax.ShapeDtypeStruct((t, LANES), F32), jax.ShapeDtypeStruct((t, hw), BF16),
                   jax.ShapeDtypeStruct((t, hw), BF16), jax.ShapeDtypeStruct((t, MLA_HEADS * MLA_V), BF16)],
        grid=(t // tm,),
        in_specs=[row(D_MODEL), _resident((1, D_MODEL))] + [_resident(w.shape) for w in in_weights]
        + [row(LANES), row(LANES)] + [_resident(c.shape) for c in consts],
        out_specs=[row(SSM_D), row(SSM_XBC), row(LANES), row(hw), row(hw), row(MLA_HEADS * MLA_V)],
        compiler_params=_params("parallel"),
        name="even_in_proj_mla_prep",
    )(x2, norm.reshape(1, D_MODEL), *in_weights, cos_full, sin_full, *consts)


def _mla_attn_body(q_ref, k_ref, v_ref, o_ref, *, blk, n_blk):
    row = lax.broadcasted_iota(jnp.int32, (blk, blk), 0)
    col = lax.broadcasted_iota(jnp.int32, (blk, blk), 1)
    causal = col <= row
    lane = lax.broadcasted_iota(jnp.int32, (blk, LANES), 1)
    for i in range(n_blk):
        rows = slice(i * blk, (i + 1) * blk)
        past = slice(0, i * blk)
        outs = []
        for hd in range(2):
            sl = slice(hd * LANES, (hd + 1) * LANES)
            q = q_ref[0, rows, sl]
            s_d = jnp.where(causal, _dot_nt(q, k_ref[0, rows, sl]), -jnp.inf)
            m = jnp.max(s_d, axis=-1, keepdims=True)
            if i:
                s_p = _dot_nt(q, k_ref[0, past, sl])
                m = jnp.maximum(m, jnp.max(s_p, axis=-1, keepdims=True))
            p_d = jnp.exp2(s_d - m)
            den = jnp.sum(p_d, axis=-1, keepdims=True)
            acc = _dot(p_d.astype(BF16), v_ref[0, rows, :])
            if i:
                p_p = jnp.exp2(s_p - m)
                den = den + jnp.sum(p_p, axis=-1, keepdims=True)
                acc = acc + _dot(p_p.astype(BF16), v_ref[0, past, :])
            outs.append(acc / den)
        o_ref[0, rows, :] = jnp.where(lane < MLA_V, outs[0], outs[1]).astype(o_ref.dtype)


def _mla_attention(q, k, v):
    b, s, _ = q.shape
    blk = min(MLA_BLOCK, s)
    pairs = MLA_HEADS // 2
    return pl.pallas_call(
        functools.partial(_mla_attn_body, blk=blk, n_blk=s // blk),
        out_shape=jax.ShapeDtypeStruct((b, s, MLA_HEADS * MLA_V), BF16),
        grid=(b, pairs),
        in_specs=[
            pl.BlockSpec((1, s, 2 * LANES), lambda bb, p: (bb, 0, p)),
            pl.BlockSpec((1, s, 2 * LANES), lambda bb, p: (bb, 0, p)),
            pl.BlockSpec((1, s, LANES), lambda bb, p: (bb, 0, p)),
        ],
        out_specs=pl.BlockSpec((1, s, LANES), lambda bb, p: (bb, 0, p)),
        compiler_params=_params("parallel", "parallel"),
        name="mla_causal_attention",
    )(q, k, v)


def _softplus(x):
    return jnp.maximum(x, 0.0) + jnp.log(1.0 + jnp.exp(-jnp.abs(x)))


def _ssd_body(xbc_ref, z_ref, dt_ref, cw_ref, cb_ref, dtb_ref, alog_ref, dsk_ref, nrm_ref,
              e64_ref, shift_ref, y_ref, xpad_ref, st_ref, *, n_chunks):
    L = SSM_CHUNK
    pad = SSD_HISTORY

    @pl.when(pl.program_id(1) == 0)
    def _():
        xpad_ref[...] = jnp.zeros_like(xpad_ref)
        st_ref[...] = jnp.zeros_like(st_ref)

    xpad_ref[pad:pad + n_chunks * L, :] = xbc_ref[0]
    convs = []
    for ci in range(n_chunks):
        taps = _dot(shift_ref[...], xpad_ref[ci * L:ci * L + SSD_WINDOW, :])
        conv = cb_ref[...]
        for back in range(SSM_CONV):
            tap = SSM_CONV - 1 - back
            conv = conv + cw_ref[tap:tap + 1, :] * taps[back * L:(back + 1) * L]
        convs.append(conv)
    xpad_ref[0:pad, :] = xpad_ref[n_chunks * L:n_chunks * L + pad, :]
    for ci in range(n_chunks):
        _ssd_chunk(convs[ci], slice(ci * L, (ci + 1) * L), z_ref, dt_ref, dtb_ref, alog_ref, dsk_ref,
                   nrm_ref, e64_ref, y_ref, st_ref)


def _ssd_chunk(conv, rows, z_ref, dt_ref, dtb_ref, alog_ref, dsk_ref, nrm_ref, e64_ref, y_ref, st_ref):
    L = SSM_CHUNK
    act = _silu(conv)
    xs = act[:, :SSM_D]
    gn = SSM_GROUPS * SSM_STATE
    b_all = act[:, SSM_D:SSM_D + gn]
    c_all = act[:, SSM_D + gn:]

    lane = lax.broadcasted_iota(jnp.int32, (L, LANES), 1)
    row = lax.broadcasted_iota(jnp.int32, (L, L), 0)
    col = lax.broadcasted_iota(jnp.int32, (L, L), 1)
    dt = jnp.where(lane < SSM_HEADS, _softplus(dt_ref[0, rows] + dtb_ref[...]), 0.0)
    a = dt * (-math.log2(math.e) * jnp.exp(alog_ref[...]))
    a_parts = _split_bf16(a, 3)
    tri = jnp.where(col <= row, 1.0, 0.0).astype(BF16)
    cs3 = _dot(tri, jnp.concatenate(a_parts, axis=1))
    cs = cs3[:, :LANES] + cs3[:, LANES:2 * LANES] + cs3[:, 2 * LANES:]
    cs_t = cs.T
    cs_last = cs[L - 1:L, :]
    ecs = jnp.exp2(cs)
    stacked = jnp.concatenate([dt, ecs, jnp.exp2(cs_last - cs)], axis=0)
    wide = _dot(jnp.concatenate(_split_bf16(stacked, 2), axis=1), e64_ref[...])
    dt_w, ecs_w, dec_w = wide[:L], wide[L:2 * L], wide[2 * L:]

    xd = xs * dt_w
    xd_b = xd.astype(BF16)
    xdw_b = (xd * dec_w).astype(BF16)
    chunk_decay = ecs_w[L - 1:L, :]

    heads_per_group = SSM_HEADS // SSM_GROUPS
    gw = heads_per_group * SSM_HEAD_DIM
    cs_col = jnp.concatenate([jnp.broadcast_to(cs[:, h:h + 1], (L, L)) for h in range(SSM_HEADS)], axis=1)
    cs_row = jnp.concatenate([jnp.broadcast_to(cs_t[h:h + 1, :], (L, L)) for h in range(SSM_HEADS)], axis=1)
    decay = jnp.exp2(jnp.minimum(cs_col - cs_row, 0.0))
    b_bf = b_all.astype(BF16)
    c_bf = c_all.astype(BF16)
    cbs = [_dot_nt(c_bf[:, g * SSM_STATE:(g + 1) * SSM_STATE], b_bf[:, g * SSM_STATE:(g + 1) * SSM_STATE])
           for g in range(SSM_GROUPS)]
    cbs = [jnp.where(col <= row, cb, 0.0) for cb in cbs]
    cb_all = jnp.concatenate([cbs[h // heads_per_group] for h in range(SSM_HEADS)], axis=1)
    mix = (cb_all * decay).astype(BF16)

    y_parts = []
    for pi in range(SSM_HEADS // 2):
        pair = xd_b[:, pi * LANES:(pi + 1) * LANES]
        zero = jnp.zeros_like(pair)
        rhs = jnp.concatenate([jnp.where(lane < SSM_HEAD_DIM, pair, zero),
                               jnp.where(lane < SSM_HEAD_DIM, zero, pair)], axis=0)
        y_parts.append(_dot(mix[:, 2 * pi * L:(2 * pi + 2) * L], rhs))
    yoff_parts = []
    for g in range(SSM_GROUPS):
        st_g = st_ref[:, g * gw:(g + 1) * gw]
        yoff_parts.append(_dot(c_bf[:, g * SSM_STATE:(g + 1) * SSM_STATE], st_g.astype(BF16)))
        bg_t = b_all[:, g * SSM_STATE:(g + 1) * SSM_STATE].T.astype(BF16)
        upd = _dot(bg_t, xdw_b[:, g * gw:(g + 1) * gw])
        st_ref[:, g * gw:(g + 1) * gw] = st_g * chunk_decay[:, g * gw:(g + 1) * gw] + upd

    y = jnp.concatenate(y_parts, axis=1) + ecs_w * jnp.concatenate(yoff_parts, axis=1)
    y = (y + dsk_ref[...] * xs) * _silu(z_ref[0, rows].astype(F32))
    for g in range(SSM_GROUPS):
        sl = slice(g * gw, (g + 1) * gw)
        y_ref[0, rows, sl] = _rms(y[:, sl], nrm_ref[:, sl]).astype(y_ref.dtype)


def _ssd(xbc, z, dt, conv_w, conv_b, dt_bias, a_log, d_skip, ssm_norm):
    b, s, _ = xbc.shape
    L = SSM_CHUNK
    heads = np.arange(2 * LANES)[:, None] % LANES
    e64 = jnp.asarray(heads == (np.arange(SSM_D)[None, :] // SSM_HEAD_DIM), BF16)
    t_idx = np.arange(SSM_CONV * L)
    shift = np.zeros((SSM_CONV * L, SSD_WINDOW), np.float32)
    shift[t_idx, SSD_HISTORY + t_idx % L - t_idx // L] = 1.0
    shift = jnp.asarray(shift, BF16)
    padh = lambda v: jnp.pad(v.reshape(1, SSM_HEADS), ((0, 0), (0, LANES - SSM_HEADS)))
    dsk = jnp.repeat(d_skip, SSM_HEAD_DIM).reshape(1, SSM_D)
    nc = SSD_CHUNKS_PER_STEP
    blk = lambda n: pl.BlockSpec((1, nc * L, n), lambda bb, c: (bb, c, 0))
    return pl.pallas_call(
        functools.partial(_ssd_body, n_chunks=nc),
        out_shape=jax.ShapeDtypeStruct((b, s, SSM_D), BF16),
        grid=(b, s // (nc * L)),
        in_specs=[blk(SSM_XBC), blk(SSM_D), blk(LANES),
                  _resident((SSM_CONV, SSM_XBC)), _resident((1, SSM_XBC)), _resident((1, LANES)),
                  _resident((1, LANES)), _resident((1, SSM_D)), _resident((1, SSM_D)),
                  _resident(e64.shape), _resident(shift.shape)],
        out_specs=blk(SSM_D),
        scratch_shapes=[pltpu.VMEM(((nc - 1) * L + SSD_WINDOW, SSM_XBC), BF16), pltpu.VMEM((SSM_STATE, SSM_D), F32)],
        compiler_params=_params("parallel", "arbitrary"),
        name="ssd_scan",
    )(xbc, z, dt, conv_w, conv_b.reshape(1, SSM_XBC), padh(dt_bias), padh(a_log), dsk,
      ssm_norm.reshape(1, SSM_D), e64, shift)


def _swa_body(q_ref, k_ref, v_ref, pc_ref, pr_ref, sink_ref, o_ref, *, tq):
    W = SWA_WINDOW
    grp = SWA_Q_HEADS // SWA_KV_HEADS
    log2e = math.log2(math.e)
    slopes = [log2e * 2.0 ** (-8.0 * (h + 1) / SWA_Q_HEADS) for h in range(SWA_Q_HEADS)]
    i = pl.program_id(1)
    lane = lax.broadcasted_iota(jnp.int32, (W, LANES), 1)
    qi = lax.broadcasted_iota(jnp.int32, (W, 2 * W), 0)
    ki = lax.broadcasted_iota(jnp.int32, (W, 2 * W), 1)
    sinks = sink_ref[...] * log2e
    sink_wide = jnp.concatenate(
        [jnp.broadcast_to(sinks[0:1, h:h + 1], (W, 2 * W)) for h in range(SWA_Q_HEADS)], axis=0)
    krow = lax.broadcasted_iota(jnp.int32, (2 * W, LANES), 0)
    ones = jnp.ones((2 * W, LANES), BF16)
    for jb in range(tq // W):
        j = i * (tq // W) + jb
        kb0 = jnp.maximum(j - 1, 0)
        ks = pl.multiple_of(kb0 * W, W)
        rel = (j - kb0) * W + qi - ki
        valid = (rel >= 0) & (rel < W)
        pq = pc_ref[0, jb * W:(jb + 1) * W, :]
        pk = jnp.concatenate([pr_ref[0, kb0], pr_ref[0, kb0 + 1]], axis=1)
        dist = jnp.where(valid, jnp.abs(pq - pk).astype(F32), jnp.inf)
        bias = jnp.concatenate([slopes[h] * dist for h in range(SWA_Q_HEADS)], axis=0)
        rows = slice(jb * W, (jb + 1) * W)
        logits = []
        for kv in range(SWA_KV_HEADS):
            kk = k_ref[0, pl.ds(ks, 2 * W), kv * LANES:(kv + 1) * LANES]
            qs = jnp.concatenate(
                [q_ref[0, rows, (kv * grp + g) * LANES:(kv * grp + g + 1) * LANES] for g in range(grp)],
                axis=0)
            logits.append(_dot_nt(qs, kk))
        s = jnp.concatenate(logits, axis=0) - bias
        sink_at = jnp.where(j == 0, 2 * W - 1, 0)
        s = jnp.where(jnp.concatenate([ki == sink_at] * SWA_Q_HEADS, axis=0), sink_wide, s)
        m = jnp.max(s, axis=-1, keepdims=True)
        pb = jnp.exp2(s - m).astype(BF16)
        outs = []
        for kv in range(SWA_KV_HEADS):
            vv = v_ref[0, pl.ds(ks, 2 * W), kv * LANES:(kv + 1) * LANES]
            vv = jnp.where(krow == sink_at, jnp.zeros_like(vv), vv)
            outs.append(_dot(pb[kv * grp * W:(kv + 1) * grp * W], jnp.concatenate([vv, ones], axis=1)))
        o = jnp.concatenate(outs, axis=0)
        o = o[:, :LANES] / o[:, LANES:]
        for pr in range(SWA_Q_HEADS // 2):
            o_pair = jnp.where(lane < SWA_HD, o[2 * pr * W:(2 * pr + 1) * W], o[(2 * pr + 1) * W:(2 * pr + 2) * W])
            o_ref[0, rows, pr * LANES:(pr + 1) * LANES] = o_pair.astype(o_ref.dtype)


def _swa(q, k, v, pos, sinks):
    b, s, _ = q.shape
    tq = min(SWA_TILE, s)
    W = SWA_WINDOW
    nb = s // W
    pos_col = pos.reshape(b, s, 1)
    pos_row = jnp.concatenate([pos.reshape(b, nb, 1, W), jnp.zeros((b, 1, 1, W), pos.dtype)], axis=1)
    return pl.pallas_call(
        functools.partial(_swa_body, tq=tq),
        out_shape=jax.ShapeDtypeStruct((b, s, SWA_Q_HEADS * SWA_HD), BF16),
        grid=(b, s // tq),
        in_specs=[
            pl.BlockSpec((1, tq, SWA_Q_HEADS * LANES), lambda bb, i: (bb, i, 0)),
            pl.BlockSpec((1, s, SWA_KV_HEADS * LANES), lambda bb, i: (bb, 0, 0)),
            pl.BlockSpec((1, s, SWA_KV_HEADS * LANES), lambda bb, i: (bb, 0, 0)),
            pl.BlockSpec((1, tq, 1), lambda bb, i: (bb, i, 0)),
            pl.BlockSpec((1, nb + 1, 1, W), lambda bb, i: (bb, 0, 0, 0)),
            _resident((1, SWA_Q_HEADS)),
        ],
        out_specs=pl.BlockSpec((1, tq, SWA_Q_HEADS * SWA_HD), lambda bb, i: (bb, i, 0)),
        compiler_params=_params("parallel", "parallel"),
        name="swa_sink_attention",
    )(q, k, v, pos_col, pos_row, sinks.reshape(1, SWA_Q_HEADS))


def _log_sigmoid(x):
    return jnp.minimum(x, 0.0) - jnp.log(1.0 + jnp.exp(-jnp.abs(x)))


def _gla_body(q_ref, k_ref, v_ref, ga_ref, gr_ref, wgb_ref, gb_ref, nrm_ref, o_ref, st_ref, *, tg):
    C = GLA_CHUNK

    @pl.when(pl.program_id(1) == 0)
    def _():
        st_ref[...] = jnp.zeros_like(st_ref)

    row = lax.broadcasted_iota(jnp.int32, (tg, tg), 0)
    col = lax.broadcasted_iota(jnp.int32, (tg, tg), 1)
    shift = C.bit_length() - 1
    same_chunk = lax.shift_right_logical(row, shift) == lax.shift_right_logical(col, shift)
    intra = same_chunk & (col <= row)
    hw = GLA_HEADS * LANES
    g = _log_sigmoid(_dot(ga_ref[0], wgb_ref[...]) + gb_ref[...]) * (math.log2(math.e) / GLA_TAU)
    masks = jnp.concatenate([jnp.where(intra, 1.0, 0.0), jnp.where(same_chunk, 1.0, 0.0)], axis=0).astype(BF16)
    sums = _dot(masks, jnp.concatenate(_split_bf16(g, 2), axis=1))
    sums = sums[:, :hw] + sums[:, hw:]
    bcum, btot = sums[:tg], sums[tg:]
    q_dec = (q_ref[0].astype(F32) * jnp.exp2(bcum)).astype(BF16)
    kf = k_ref[0].astype(F32)
    k_inv = (kf * jnp.exp2(-bcum)).astype(BF16)
    k_end = (kf * jnp.exp2(btot - bcum)).astype(BF16)
    chunk_dec = jnp.exp2(btot)
    n_c = tg // C
    heads = [slice(hd * LANES, (hd + 1) * LANES) for hd in range(GLA_HEADS)]
    chunks = [slice(c * C, (c + 1) * C) for c in range(n_c)]
    vs = [v_ref[0, :, sl] for sl in heads]
    o_intra = [_dot(jnp.where(intra, _dot_nt(q_dec[:, sl], k_inv[:, sl]), 0.0).astype(BF16), vh)
               for sl, vh in zip(heads, vs)]
    kvs = [[_dot_tn(vh[rows], k_end[rows, sl]) for rows in chunks] for sl, vh in zip(heads, vs)]
    for hd, sl in enumerate(heads):
        st = st_ref[hd]
        inter = []
        for c, rows in enumerate(chunks):
            inter.append(_dot_nt(q_dec[rows, sl], st.astype(BF16)))
            st = st * chunk_dec[c * C:c * C + 1, sl] + kvs[hd][c]
        st_ref[hd] = st
        o = o_intra[hd] + jnp.concatenate(inter, axis=0)
        o = _rms(o, nrm_ref[...]) * _silu(gr_ref[0, :, sl].astype(F32))
        o_ref[0, :, sl] = o.astype(o_ref.dtype)


def _gla(q, k, v, ga, gr, wgb, gate_bias, gla_norm):
    b, s, _ = q.shape
    tg = min(GLA_TILE, s)
    hw = GLA_HEADS * LANES
    blk = lambda n: pl.BlockSpec((1, tg, n), lambda bb, i: (bb, i, 0))
    return pl.pallas_call(
        functools.partial(_gla_body, tg=tg),
        out_shape=jax.ShapeDtypeStruct((b, s, GLA_HEADS * GLA_DV), BF16),
        grid=(b, s // tg),
        in_specs=[blk(hw), blk(hw), blk(GLA_HEADS * GLA_DV), blk(LANES), blk(GLA_HEADS * GLA_DV),
                  _resident(wgb.shape), _resident((1, hw)), _resident((1, GLA_DV))],
        out_specs=blk(GLA_HEADS * GLA_DV),
        scratch_shapes=[pltpu.VMEM((GLA_HEADS, GLA_DV, LANES), F32)],
        compiler_params=_params("parallel", "arbitrary"),
        name="gla_chunked",
    )(q, k, v, ga, gr, wgb, gate_bias, gla_norm.reshape(1, GLA_DV))


def _pad_heads(w, n_heads, width, lanes=LANES, offset=0):
    lead = w.shape[:-1]
    w = w.reshape(lead + (n_heads, width))
    cfg = [(0, 0)] * len(lead) + [(0, 0), (offset, lanes - width - offset)]
    return jnp.pad(w, cfg).reshape(lead + (n_heads * lanes,))


def _rot_cols(w):
    half = MLA_ROPE // 2
    lane = np.arange(LANES)
    src = np.where((lane >= MLA_NOPE) & (lane < MLA_NOPE + half), lane + half,
                   np.where((lane >= MLA_NOPE + half) & (lane < MLA_QK), lane - half, 0))
    keep = jnp.asarray((lane >= MLA_NOPE) & (lane < MLA_QK), w.dtype)
    blocks = w.reshape(w.shape[:-1] + (w.shape[-1] // LANES, LANES))
    return (blocks[..., src] * keep).reshape(w.shape)


def _pad_cols(w, total, offset=0):
    return jnp.pad(w, [(0, 0)] * (w.ndim - 1) + [(offset, total - w.shape[-1] - offset)])


def _even_mixer(x2, b, s, pos, mix_norm, w_in, conv_w, conv_b, dt_bias, a_log, d_skip, ssm_norm,
                q_a_norm, w_q_b, kv_a_norm, w_kv_b, q_norm, k_norm, w_out):
    t = b * s
    o0 = 0
    cols = {}
    for name, n in (("z", SSM_D), ("xbc", SSM_XBC), ("dt", SSM_HEADS), ("qa", MLA_Q_RANK),
                    ("kvc", MLA_KV_RANK), ("kpe", MLA_ROPE)):
        cols[name] = w_in[:, o0:o0 + n]
        o0 += n
    w_kpe = _pad_cols(cols["kpe"], LANES, offset=MLA_NOPE)
    weights = [cols["z"], cols["xbc"], _pad_cols(cols["dt"], LANES), cols["qa"], cols["kvc"],
               jnp.concatenate([w_kpe, _rot_cols(w_kpe)], axis=1)]
    weights = [w.astype(BF16) for w in weights]
    cos_full, sin_full = _rope_tables(pos)
    wq = _pad_heads(w_q_b, MLA_HEADS, MLA_QK)
    wq = jnp.concatenate([wq, _rot_cols(wq)], axis=1).astype(BF16)
    with_rot = lambda g: jnp.concatenate([g, _rot_cols(g)], axis=1)
    w_kv = w_kv_b.reshape(MLA_KV_RANK, MLA_HEADS, MLA_NOPE + MLA_V)
    wk = _pad_heads(w_kv[:, :, :MLA_NOPE].reshape(MLA_KV_RANK, -1), MLA_HEADS, MLA_NOPE).astype(BF16)
    wv = w_kv[:, :, MLA_NOPE:].reshape(MLA_KV_RANK, MLA_HEADS * MLA_V).astype(BF16)
    z, xbc, dt, q, k, v = _even_inproj(x2, mix_norm, weights, cos_full, sin_full, q_a_norm.reshape(1, -1),
                                       kv_a_norm.reshape(1, -1), wq, wk, wv,
                                       with_rot(_pad_cols(q_norm.reshape(1, -1), LANES)),
                                       with_rot(_pad_cols(k_norm.reshape(1, -1), LANES)))
    y = _ssd(xbc.reshape(b, s, -1), z.reshape(b, s, -1), dt.reshape(b, s, -1), conv_w, conv_b,
             dt_bias, a_log, d_skip, ssm_norm).reshape(t, SSM_D)
    o = _mla_attention(q.reshape(b, s, -1), k.reshape(b, s, -1), v.reshape(b, s, -1)).reshape(t, -1)
    return [y, o], [w_out[:SSM_D], w_out[SSM_D:]]


def _odd_mixer(x2, b, s, pos, mix_norm, w_in, q_norm, k_norm, sinks, w_gate_b, gate_bias, gla_norm,
               w_out):
    t = b * s
    sizes = [SWA_Q_HEADS * SWA_HD, SWA_KV_HEADS * SWA_HD, SWA_KV_HEADS * SWA_HD,
             GLA_HEADS * GLA_DK, GLA_HEADS * GLA_DK, GLA_HEADS * GLA_DV, GLA_RANK, GLA_HEADS * GLA_DV]
    parts, o0 = [], 0
    for n in sizes:
        parts.append(w_in[:, o0:o0 + n])
        o0 += n
    wq, wk, wv, wgq, wgk, wgv, wga, wgr = parts
    v_dup = jnp.repeat(wv.reshape(D_MODEL, SWA_KV_HEADS, 1, SWA_HD), 2, axis=2).reshape(D_MODEL, -1)
    weights = [_pad_heads(wq, SWA_Q_HEADS, SWA_HD), _pad_heads(wk, SWA_KV_HEADS, SWA_HD), v_dup,
               _pad_heads(wgq * (GLA_DK ** -0.5), GLA_HEADS, GLA_DK), _pad_heads(wgk, GLA_HEADS, GLA_DK),
               wgv, _pad_cols(wga, LANES), wgr]
    weights = [w.astype(BF16) for w in weights]
    gains = [_pad_cols(q_norm.reshape(1, -1) * (SWA_HD ** -0.5 * math.log2(math.e)), LANES),
             _pad_cols(k_norm.reshape(1, -1), LANES)]
    q, k, v, gq, gk, gv, ga, gr = _inproj(x2, mix_norm, weights, [BF16] * 8,
                                          [SWA_HD, SWA_HD, 0, 0, 0, 0, 0, 0], gains)
    o_swa = _swa(q.reshape(b, s, -1), k.reshape(b, s, -1), v.reshape(b, s, -1), pos, sinks)
    wgb = _pad_heads(jnp.pad(w_gate_b, ((0, LANES - GLA_RANK), (0, 0))), GLA_HEADS, GLA_DK).astype(BF16)
    gb = _pad_heads(gate_bias.reshape(1, -1), GLA_HEADS, GLA_DK)
    o_gla = _gla(gq.reshape(b, s, -1), gk.reshape(b, s, -1), gv.reshape(b, s, -1), ga.reshape(b, s, -1),
                 gr.reshape(b, s, -1), wgb, gb, gla_norm)
    n_swa = SWA_Q_HEADS * SWA_HD
    return [o_swa.reshape(t, -1), o_gla.reshape(t, -1)], [w_out[:n_swa], w_out[n_swa:]]


def kernel(x, positions, pre_norm, pre_w_gate, pre_w_up, pre_w_down, mix_norm, post_norm, post_w_gate,
           post_w_up, post_w_down, e_w_in, e_conv_w, e_conv_b, e_dt_bias, e_a_log, e_d_skip, e_ssm_norm,
           e_q_a_norm, e_w_q_b, e_kv_a_norm, e_w_kv_b, e_q_norm, e_k_norm, e_w_out, o_w_in, o_q_norm,
           o_k_norm, o_sinks, o_w_gate_b, o_gate_bias, o_gla_norm, o_w_out):
    b, s, d = x.shape
    depth = pre_norm.shape[0]
    x2 = x.reshape(b * s, d)
    for layer in range(depth):
        x2 = _ffn(x2, pre_norm[layer], pre_w_gate[layer], pre_w_up[layer], pre_w_down[layer])
        j = layer // 2
        if layer % 2 == 0:
            acts, w_outs = _even_mixer(x2, b, s, positions, mix_norm[layer], e_w_in[j], e_conv_w[j],
                                       e_conv_b[j], e_dt_bias[j], e_a_log[j], e_d_skip[j], e_ssm_norm[j],
                                       e_q_a_norm[j], e_w_q_b[j], e_kv_a_norm[j], e_w_kv_b[j], e_q_norm[j],
                                       e_k_norm[j], e_w_out[j])
        else:
            acts, w_outs = _odd_mixer(x2, b, s, positions, mix_norm[layer], o_w_in[j], o_q_norm[j],
                                      o_k_norm[j], o_sinks[j], o_w_gate_b[j], o_gate_bias[j],
                                      o_gla_norm[j], o_w_out[j])
        x2 = _ffn(x2, post_norm[layer], post_w_gate[layer], post_w_up[layer], post_w_down[layer],
                  acts, w_outs)
    return x2.reshape(b, s, d)
```

```python
import functools
import math

import jax
import jax.numpy as jnp
import numpy as np
from jax import lax
from jax.experimental import pallas as pl
from jax.experimental.pallas import tpu as pltpu

F32 = jnp.float32
BF16 = jnp.bfloat16

D_MODEL = 1024
D_FF = 2816
FFN_RES = 0.5
EPS = 1e-6
SSM_HEADS = 16
SSM_HEAD_DIM = 64
SSM_D = SSM_HEADS * SSM_HEAD_DIM
SSM_GROUPS = 4
SSM_STATE = 128
SSM_CONV = 4
SSM_CHUNK = 128
SSM_XBC = SSM_D + 2 * SSM_GROUPS * SSM_STATE
MLA_HEADS = 8
MLA_Q_RANK = 384
MLA_KV_RANK = 256
MLA_NOPE = 64
MLA_ROPE = 32
MLA_QK = MLA_NOPE + MLA_ROPE
MLA_V = 64
ROPE_THETA = 10000.0
SWA_Q_HEADS = 8
SWA_KV_HEADS = 2
SWA_HD = 64
SWA_WINDOW = 128
GLA_HEADS = 4
GLA_DK = 64
GLA_DV = 128
GLA_RANK = 16
GLA_TAU = 16.0
GLA_CHUNK = 64

LANES = 128
VMEM_LIMIT_BYTES = 56 * 1024 * 1024

FF_CHUNK = 256
TOKEN_TILE = 1024
FFN_TILE = 1024
MLA_BLOCK = 512
SWA_TILE = 1024
GLA_TILE = 256
SSD_HISTORY = 16
SSD_WINDOW = 256
SSD_CHUNKS_PER_STEP = 4


def _params(*sem):
    return pltpu.CompilerParams(dimension_semantics=sem, vmem_limit_bytes=VMEM_LIMIT_BYTES)


def _resident(shape):
    nd = len(shape)
    return pl.BlockSpec(shape, lambda *_: (0,) * nd, pipeline_mode=pl.Buffered(1))


def _rms(x, gain):
    return x * lax.rsqrt(jnp.mean(x * x, axis=-1, keepdims=True) + EPS) * gain


def _silu(x):
    return x * jax.nn.sigmoid(x)


def _dot(a, b):
    return jnp.dot(a, b, preferred_element_type=F32)


def _dot_nt(a, b):
    return lax.dot_general(a, b, (((1,), (1,)), ((), ())), preferred_element_type=F32)


def _split_bf16(x, n):
    parts = []
    for _ in range(n - 1):
        p = x.astype(BF16)
        parts.append(p)
        x = x - p.astype(F32)
    parts.append(x.astype(BF16))
    return parts


def _dot_tn(a, b):
    return lax.dot_general(a, b, (((0,), (0,)), ((), ())), preferred_element_type=F32)


def _ffn_body(x_ref, g_ref, wg_ref, wu_ref, wd_ref, *refs, n_mix):
    mix_a, mix_w = refs[:n_mix], refs[n_mix:2 * n_mix]
    o_ref, h_ref, acc_ref = refs[2 * n_mix:]
    x = x_ref[...]
    if n_mix:
        mix = _dot(mix_a[0][...], mix_w[0][...])
        for a_ref, w_ref in zip(mix_a[1:], mix_w[1:]):
            mix = mix + _dot(a_ref[...], w_ref[...])
        x = x + mix
    acc_ref[...] = x
    h_ref[...] = _rms(acc_ref[...], g_ref[...]).astype(BF16)
    for c in range(D_FF // FF_CHUNK):
        cols = slice(c * FF_CHUNK, (c + 1) * FF_CHUNK)
        gate = _dot(h_ref[...], wg_ref[:, cols])
        up = _dot(h_ref[...], wu_ref[:, cols])
        acc_ref[...] += _dot((_silu(gate) * up).astype(BF16), wd_ref[cols, :])
    o_ref[...] = acc_ref[...]


def _ffn(x2, norm, w_gate, w_up, w_down, mix_acts=(), mix_weights=()):
    t = x2.shape[0]
    tm = min(FFN_TILE, t)
    weights = [w_gate.astype(BF16), w_up.astype(BF16), (FFN_RES * w_down).astype(BF16)]
    mix_weights = [w.astype(BF16) for w in mix_weights]
    row = lambda n: pl.BlockSpec((tm, n), lambda i: (i, 0))
    return pl.pallas_call(
        functools.partial(_ffn_body, n_mix=len(mix_acts)),
        out_shape=jax.ShapeDtypeStruct((t, D_MODEL), F32),
        grid=(t // tm,),
        in_specs=[row(D_MODEL), _resident((1, D_MODEL))] + [_resident(w.shape) for w in weights]
        + [row(a.shape[1]) for a in mix_acts] + [_resident(w.shape) for w in mix_weights],
        out_specs=row(D_MODEL),
        scratch_shapes=[pltpu.VMEM((tm, D_MODEL), BF16), pltpu.VMEM((tm, D_MODEL), F32)],
        compiler_params=_params("parallel"),
        name="swiglu_half_step",
    )(x2, norm.reshape(1, D_MODEL), *weights, *mix_acts, *mix_weights)


def _head_norm(t, gain, n_valid):
    ms = jnp.sum(t * t, axis=-1, keepdims=True) * (1.0 / n_valid)
    return t * lax.rsqrt(ms + EPS) * gain


def _inproj_body(x_ref, g_ref, *refs, head_norms):
    n_out = len(head_norms)
    w_refs = refs[:n_out]
    n_gain = sum(1 for hn in head_norms if hn)
    gain_refs = refs[n_out:n_out + n_gain]
    o_refs = refs[n_out + n_gain:]
    h = _rms(x_ref[...], g_ref[...]).astype(BF16)
    gi = 0
    for w_ref, o_ref, hn in zip(w_refs, o_refs, head_norms):
        y = _dot(h, w_ref[...])
        if hn:
            gain = gain_refs[gi][...]
            gi += 1
            for hd in range(y.shape[1] // LANES):
                sl = slice(hd * LANES, (hd + 1) * LANES)
                o_ref[:, sl] = _head_norm(y[:, sl], gain, hn).astype(o_ref.dtype)
        else:
            o_ref[...] = y.astype(o_ref.dtype)


def _inproj(x2, norm, weights, out_dtypes, head_norms, gains):
    t = x2.shape[0]
    tm = min(TOKEN_TILE, t)
    in_specs = [pl.BlockSpec((tm, D_MODEL), lambda i: (i, 0)), _resident((1, D_MODEL))]
    in_specs += [_resident(w.shape) for w in weights]
    in_specs += [_resident(g.shape) for g in gains]
    return pl.pallas_call(
        functools.partial(_inproj_body, head_norms=tuple(head_norms)),
        out_shape=[jax.ShapeDtypeStruct((t, w.shape[1]), dt) for w, dt in zip(weights, out_dtypes)],
        grid=(t // tm,),
        in_specs=in_specs,
        out_specs=[pl.BlockSpec((tm, w.shape[1]), lambda i: (i, 0)) for w in weights],
        compiler_params=_params("parallel"),
        name="mixer_in_proj",
    )(x2, norm.reshape(1, D_MODEL), *weights, *gains)


def _rope_body(pos_ref, inv_ref, spread_ref, base_ref, cos_ref, sin_ref):
    ang = pos_ref[...] * inv_ref[...]
    terms = _split_bf16(jnp.cos(ang), 3) + _split_bf16(jnp.sin(ang), 3)
    lhs = jnp.concatenate(terms, axis=1)
    per_row = LANES // (MLA_ROPE // 2)
    tr = lhs.shape[0]
    for p in range(per_row):
        wide = _dot(lhs, spread_ref[p])
        cos_ref[pl.ds(p, tr, stride=per_row), :] = wide[:, :LANES] + base_ref[...]
        sin_ref[pl.ds(p, tr, stride=per_row), :] = wide[:, LANES:]


def _rope_tables(positions):
    b, s = positions.shape
    half = MLA_ROPE // 2
    inv = ROPE_THETA ** (-jnp.arange(0, MLA_ROPE, 2, dtype=F32) / MLA_ROPE)
    per_row = LANES // half
    rows = b * s // per_row
    pos_rep = jnp.repeat(positions.astype(F32).reshape(rows, per_row), half, axis=1)
    inv_row = jnp.tile(inv, per_row).reshape(1, LANES)
    n_terms = 3
    spread = np.zeros((per_row, 2 * n_terms * LANES, 2 * LANES), np.float32)
    f = np.arange(half)
    for p in range(per_row):
        for part in range(n_terms):
            src_c = part * LANES + p * half + f
            src_s = (n_terms + part) * LANES + p * half + f
            spread[p, src_c, MLA_NOPE + f] = 1.0
            spread[p, src_c, MLA_NOPE + half + f] = 1.0
            spread[p, src_s, LANES + MLA_NOPE + f] = -1.0
            spread[p, src_s, LANES + MLA_NOPE + half + f] = 1.0
    lane = np.arange(LANES)
    base = np.where((lane >= MLA_NOPE) & (lane < MLA_QK), 0.0, 1.0).astype(np.float32).reshape(1, LANES)
    tr = min(1024, rows)
    return pl.pallas_call(
        _rope_body,
        out_shape=[jax.ShapeDtypeStruct((b * s, LANES), F32)] * 2,
        grid=(rows // tr,),
        in_specs=[pl.BlockSpec((tr, LANES), lambda i: (i, 0)), _resident((1, LANES)),
                  _resident(spread.shape), _resident((1, LANES))],
        out_specs=[pl.BlockSpec((tr * per_row, LANES), lambda i: (i, 0))] * 2,
        compiler_params=_params("parallel"),
        name="rope_tables",
    )(pos_rep, inv_row, jnp.asarray(spread, BF16), jnp.asarray(base))


def _even_inproj_body(x_ref, g_ref, wz_ref, wxbc_ref, wdt_ref, wqa_ref, wkvc_ref, wkpe_ref, cos_ref, sin_ref,
                      qan_ref, kvn_ref, wq_ref, wk_ref, wv_ref, qn_ref, kn_ref,
                      z_ref, xbc_ref, dt_ref, q_ref, k_ref, v_ref):
    h = _rms(x_ref[...], g_ref[...]).astype(BF16)
    z_ref[...] = _dot(h, wz_ref[...]).astype(z_ref.dtype)
    xbc_ref[...] = _dot(h, wxbc_ref[...]).astype(xbc_ref.dtype)
    dt_ref[...] = _dot(h, wdt_ref[...])
    qa = _rms(_dot(h, wqa_ref[...]), qan_ref[...]).astype(BF16)
    kvc = _rms(_dot(h, wkvc_ref[...]), kvn_ref[...]).astype(BF16)
    kpe2 = _dot(h, wkpe_ref[...])

    hw = MLA_HEADS * LANES
    cosf = cos_ref[...]
    sinf = sin_ref[...]
    qscale = MLA_QK ** -0.5 * math.log2(math.e)
    q_cos = (qn_ref[:, :LANES] * qscale) * cosf
    q_sin = (qn_ref[:, LANES:] * qscale) * sinf
    k_cos = kn_ref[:, :LANES] * cosf
    kpe = kpe2[:, :LANES]
    k_rot = kpe2[:, LANES:] * (kn_ref[:, LANES:] * sinf)

    def inv_rms(t):
        return lax.rsqrt(jnp.sum(t * t, axis=-1, keepdims=True) * (1.0 / MLA_QK) + EPS)

    q_all = _dot(qa, wq_ref[...])
    k_all = _dot(kvc, wk_ref[...])
    v_ref[...] = _dot(kvc, wv_ref[...]).astype(v_ref.dtype)
    for hd in range(MLA_HEADS):
        sl = slice(hd * LANES, (hd + 1) * LANES)
        yq = q_all[:, sl]
        yq_rot = q_all[:, hw + hd * LANES:hw + (hd + 1) * LANES]
        q_ref[:, sl] = ((yq * q_cos + yq_rot * q_sin) * inv_rms(yq)).astype(q_ref.dtype)
        yk = k_all[:, sl] + kpe
        k_ref[:, sl] = ((yk * k_cos + k_rot) * inv_rms(yk)).astype(k_ref.dtype)


def _even_inproj(x2, norm, in_weights, cos_full, sin_full, qan, kvn, wq, wk, wv, qn, kn):
    t = x2.shape[0]
    tm = min(TOKEN_TILE, t)
    row = lambda n: pl.BlockSpec((tm, n), lambda i: (i, 0))
    hw = MLA_HEADS * LANES
    consts = [qan, kvn, wq, wk, wv, qn, kn]
    return pl.pallas_call(
        _even_inproj_body,
        out_shape=[jax.ShapeDtypeStruct((t, SSM_D), BF16), jax.ShapeDtypeStruct((t, SSM_XBC), BF16),
                   jax.ShapeDtypeStruct((t, LANES), F32), jax.ShapeDtypeStruct((t, hw), BF16),
                   jax.ShapeDtypeStruct((t, hw), BF16), jax.ShapeDtypeStruct((t, MLA_HEADS * MLA_V), BF16)],
        grid=(t // tm,),
        in_specs=[row(D_MODEL), _resident((1, D_MODEL))] + [_resident(w.shape) for w in in_weights]
        + [row(LANES), row(LANES)] + [_resident(c.shape) for c in consts],
        out_specs=[row(SSM_D), row(SSM_XBC), row(LANES), row(hw), row(hw), row(MLA_HEADS * MLA_V)],
        compiler_params=_params("parallel"),
        name="even_in_proj_mla_prep",
    )(x2, norm.reshape(1, D_MODEL), *in_weights, cos_full, sin_full, *consts)


def _mla_attn_body(q_ref, k_ref, v_ref, o_ref, *, blk, n_blk):
    row = lax.broadcasted_iota(jnp.int32, (blk, blk), 0)
    col = lax.broadcasted_iota(jnp.int32, (blk, blk), 1)
    causal = col <= row
    lane = lax.broadcasted_iota(jnp.int32, (blk, LANES), 1)
    ones = jnp.ones((blk, LANES), BF16)
    for i in range(n_blk):
        rows = slice(i * blk, (i + 1) * blk)
        past = slice(0, i * blk)
        outs = []
        for hd in range(2):
            sl = slice(hd * LANES, (hd + 1) * LANES)
            q = q_ref[0, rows, sl]
            s_d = jnp.where(causal, _dot_nt(q, k_ref[0, rows, sl]), -jnp.inf)
            m = jnp.max(s_d, axis=-1, keepdims=True)
            if i:
                s_p = _dot_nt(q, k_ref[0, past, sl])
                m = jnp.maximum(m, jnp.max(s_p, axis=-1, keepdims=True))
            acc = _dot(jnp.exp2(s_d - m).astype(BF16), jnp.concatenate([v_ref[0, rows, :], ones], axis=1))
            if i:
                acc = acc + _dot(jnp.exp2(s_p - m).astype(BF16),
                                 jnp.concatenate([v_ref[0, past, :], jnp.ones((i * blk, LANES), BF16)], axis=1))
            outs.append(acc[:, :LANES] / acc[:, LANES:])
        o_ref[0, rows, :] = jnp.where(lane < MLA_V, outs[0], outs[1]).astype(o_ref.dtype)


def _mla_attention(q, k, v):
    b, s, _ = q.shape
    blk = min(MLA_BLOCK, s)
    pairs = MLA_HEADS // 2
    return pl.pallas_call(
        functools.partial(_mla_attn_body, blk=blk, n_blk=s // blk),
        out_shape=jax.ShapeDtypeStruct((b, s, MLA_HEADS * MLA_V), BF16),
        grid=(b, pairs),
        in_specs=[
            pl.BlockSpec((1, s, 2 * LANES), lambda bb, p: (bb, 0, p)),
            pl.BlockSpec((1, s, 2 * LANES), lambda bb, p: (bb, 0, p)),
            pl.BlockSpec((1, s, LANES), lambda bb, p: (bb, 0, p)),
        ],
        out_specs=pl.BlockSpec((1, s, LANES), lambda bb, p: (bb, 0, p)),
        compiler_params=_params("parallel", "parallel"),
        name="mla_causal_attention",
    )(q, k, v)


def _softplus(x):
    return jnp.maximum(x, 0.0) + jnp.log(1.0 + jnp.exp(-jnp.abs(x)))


def _ssd_body(xbc_ref, z_ref, dt_ref, cw_ref, cb_ref, dtb_ref, alog_ref, dsk_ref, nrm_ref,
              e64_ref, shift_ref, y_ref, xpad_ref, st_ref, *, n_chunks):
    L = SSM_CHUNK
    pad = SSD_HISTORY

    @pl.when(pl.program_id(1) == 0)
    def _():
        xpad_ref[...] = jnp.zeros_like(xpad_ref)
        st_ref[...] = jnp.zeros_like(st_ref)

    xpad_ref[pad:pad + n_chunks * L, :] = xbc_ref[0]
    convs = []
    for ci in range(n_chunks):
        taps = _dot(shift_ref[...], xpad_ref[ci * L:ci * L + SSD_WINDOW, :])
        conv = cb_ref[...]
        for back in range(SSM_CONV):
            tap = SSM_CONV - 1 - back
            conv = conv + cw_ref[tap:tap + 1, :] * taps[back * L:(back + 1) * L]
        convs.append(conv)
    xpad_ref[0:pad, :] = xpad_ref[n_chunks * L:n_chunks * L + pad, :]
    for ci in range(n_chunks):
        _ssd_chunk(convs[ci], slice(ci * L, (ci + 1) * L), z_ref, dt_ref, dtb_ref, alog_ref, dsk_ref,
                   nrm_ref, e64_ref, y_ref, st_ref)


def _ssd_chunk(conv, rows, z_ref, dt_ref, dtb_ref, alog_ref, dsk_ref, nrm_ref, e64_ref, y_ref, st_ref):
    L = SSM_CHUNK
    act = _silu(conv)
    xs = act[:, :SSM_D]
    gn = SSM_GROUPS * SSM_STATE
    b_all = act[:, SSM_D:SSM_D + gn]
    c_all = act[:, SSM_D + gn:]

    lane = lax.broadcasted_iota(jnp.int32, (L, LANES), 1)
    row = lax.broadcasted_iota(jnp.int32, (L, L), 0)
    col = lax.broadcasted_iota(jnp.int32, (L, L), 1)
    dt = jnp.where(lane < SSM_HEADS, _softplus(dt_ref[0, rows] + dtb_ref[...]), 0.0)
    a = dt * (-math.log2(math.e) * jnp.exp(alog_ref[...]))
    a_parts = _split_bf16(a, 3)
    tri = jnp.where(col <= row, 1.0, 0.0).astype(BF16)
    cs3 = _dot(tri, jnp.concatenate(a_parts, axis=1))
    cs = cs3[:, :LANES] + cs3[:, LANES:2 * LANES] + cs3[:, 2 * LANES:]
    cs_t = cs.T
    cs_last = cs[L - 1:L, :]
    ecs = jnp.exp2(cs)
    stacked = jnp.concatenate([dt, ecs, jnp.exp2(cs_last - cs)], axis=0)
    wide = _dot(jnp.concatenate(_split_bf16(stacked, 2), axis=1), e64_ref[...])
    dt_w, ecs_w, dec_w = wide[:L], wide[L:2 * L], wide[2 * L:]

    xd = xs * dt_w
    xd_b = xd.astype(BF16)
    xdw_b = (xd * dec_w).astype(BF16)
    chunk_decay = ecs_w[L - 1:L, :]

    heads_per_group = SSM_HEADS // SSM_GROUPS
    gw = heads_per_group * SSM_HEAD_DIM
    cs_col = jnp.concatenate([jnp.broadcast_to(cs[:, h:h + 1], (L, L)) for h in range(SSM_HEADS)], axis=1)
    cs_row = jnp.concatenate([jnp.broadcast_to(cs_t[h:h + 1, :], (L, L)) for h in range(SSM_HEADS)], axis=1)
    decay = jnp.exp2(jnp.minimum(cs_col - cs_row, 0.0))
    b_bf = b_all.astype(BF16)
    c_bf = c_all.astype(BF16)
    cbs = [_dot_nt(c_bf[:, g * SSM_STATE:(g + 1) * SSM_STATE], b_bf[:, g * SSM_STATE:(g + 1) * SSM_STATE])
           for g in range(SSM_GROUPS)]
    cbs = [jnp.where(col <= row, cb, 0.0) for cb in cbs]
    cb_all = jnp.concatenate([cbs[h // heads_per_group] for h in range(SSM_HEADS)], axis=1)
    mix = (cb_all * decay).astype(BF16)

    y_parts = []
    for pi in range(SSM_HEADS // 2):
        pair = xd_b[:, pi * LANES:(pi + 1) * LANES]
        zero = jnp.zeros_like(pair)
        rhs = jnp.concatenate([jnp.where(lane < SSM_HEAD_DIM, pair, zero),
                               jnp.where(lane < SSM_HEAD_DIM, zero, pair)], axis=0)
        y_parts.append(_dot(mix[:, 2 * pi * L:(2 * pi + 2) * L], rhs))
    yoff_parts = []
    for g in range(SSM_GROUPS):
        st_g = st_ref[:, g * gw:(g + 1) * gw]
        yoff_parts.append(_dot(c_bf[:, g * SSM_STATE:(g + 1) * SSM_STATE], st_g.astype(BF16)))
        bg_t = b_all[:, g * SSM_STATE:(g + 1) * SSM_STATE].T.astype(BF16)
        upd = _dot(bg_t, xdw_b[:, g * gw:(g + 1) * gw])
        st_ref[:, g * gw:(g + 1) * gw] = st_g * chunk_decay[:, g * gw:(g + 1) * gw] + upd

    y = jnp.concatenate(y_parts, axis=1) + ecs_w * jnp.concatenate(yoff_parts, axis=1)
    y = (y + dsk_ref[...] * xs) * _silu(z_ref[0, rows].astype(F32))
    for g in range(SSM_GROUPS):
        sl = slice(g * gw, (g + 1) * gw)
        y_ref[0, rows, sl] = _rms(y[:, sl], nrm_ref[:, sl]).astype(y_ref.dtype)


def _ssd(xbc, z, dt, conv_w, conv_b, dt_bias, a_log, d_skip, ssm_norm):
    b, s, _ = xbc.shape
    L = SSM_CHUNK
    heads = np.arange(2 * LANES)[:, None] % LANES
    e64 = jnp.asarray(heads == (np.arange(SSM_D)[None, :] // SSM_HEAD_DIM), BF16)
    t_idx = np.arange(SSM_CONV * L)
    shift = np.zeros((SSM_CONV * L, SSD_WINDOW), np.float32)
    shift[t_idx, SSD_HISTORY + t_idx % L - t_idx // L] = 1.0
    shift = jnp.asarray(shift, BF16)
    padh = lambda v: jnp.pad(v.reshape(1, SSM_HEADS), ((0, 0), (0, LANES - SSM_HEADS)))
    dsk = jnp.repeat(d_skip, SSM_HEAD_DIM).reshape(1, SSM_D)
    nc = SSD_CHUNKS_PER_STEP
    blk = lambda n: pl.BlockSpec((1, nc * L, n), lambda bb, c: (bb, c, 0))
    return pl.pallas_call(
        functools.partial(_ssd_body, n_chunks=nc),
        out_shape=jax.ShapeDtypeStruct((b, s, SSM_D), BF16),
        grid=(b, s // (nc * L)),
        in_specs=[blk(SSM_XBC), blk(SSM_D), blk(LANES),
                  _resident((SSM_CONV, SSM_XBC)), _resident((1, SSM_XBC)), _resident((1, LANES)),
                  _resident((1, LANES)), _resident((1, SSM_D)), _resident((1, SSM_D)),
                  _resident(e64.shape), _resident(shift.shape)],
        out_specs=blk(SSM_D),
        scratch_shapes=[pltpu.VMEM(((nc - 1) * L + SSD_WINDOW, SSM_XBC), BF16), pltpu.VMEM((SSM_STATE, SSM_D), F32)],
        compiler_params=_params("parallel", "arbitrary"),
        name="ssd_scan",
    )(xbc, z, dt, conv_w, conv_b.reshape(1, SSM_XBC), padh(dt_bias), padh(a_log), dsk,
      ssm_norm.reshape(1, SSM_D), e64, shift)


def _swa_body(q_ref, k_ref, v_ref, pc_ref, pr_ref, sink_ref, o_ref, *, tq):
    W = SWA_WINDOW
    grp = SWA_Q_HEADS // SWA_KV_HEADS
    log2e = math.log2(math.e)
    slopes = [log2e * 2.0 ** (-8.0 * (h + 1) / SWA_Q_HEADS) for h in range(SWA_Q_HEADS)]
    i = pl.program_id(1)
    lane = lax.broadcasted_iota(jnp.int32, (W, LANES), 1)
    qi = lax.broadcasted_iota(jnp.int32, (W, 2 * W), 0)
    ki = lax.broadcasted_iota(jnp.int32, (W, 2 * W), 1)
    sinks = sink_ref[...] * log2e
    sink_wide = jnp.concatenate(
        [jnp.broadcast_to(sinks[0:1, h:h + 1], (W, 2 * W)) for h in range(SWA_Q_HEADS)], axis=0)
    krow = lax.broadcasted_iota(jnp.int32, (2 * W, LANES), 0)
    ones = jnp.ones((2 * W, LANES), BF16)
    for jb in range(tq // W):
        j = i * (tq // W) + jb
        kb0 = jnp.maximum(j - 1, 0)
        ks = pl.multiple_of(kb0 * W, W)
        rel = (j - kb0) * W + qi - ki
        valid = (rel >= 0) & (rel < W)
        pq = pc_ref[0, jb * W:(jb + 1) * W, :]
        pk = jnp.concatenate([pr_ref[0, kb0], pr_ref[0, kb0 + 1]], axis=1)
        dist = jnp.where(valid, jnp.abs(pq - pk).astype(F32), jnp.inf)
        bias = jnp.concatenate([slopes[h] * dist for h in range(SWA_Q_HEADS)], axis=0)
        rows = slice(jb * W, (jb + 1) * W)
        logits = []
        for kv in range(SWA_KV_HEADS):
            kk = k_ref[0, pl.ds(ks, 2 * W), kv * LANES:(kv + 1) * LANES]
            qs = jnp.concatenate(
                [q_ref[0, rows, (kv * grp + g) * LANES:(kv * grp + g + 1) * LANES] for g in range(grp)],
                axis=0)
            logits.append(_dot_nt(qs, kk))
        s = jnp.concatenate(logits, axis=0) - bias
        sink_at = jnp.where(j == 0, 2 * W - 1, 0)
        s = jnp.where(jnp.concatenate([ki == sink_at] * SWA_Q_HEADS, axis=0), sink_wide, s)
        m = jnp.max(s, axis=-1, keepdims=True)
        pb = jnp.exp2(s - m).astype(BF16)
        outs = []
        for kv in range(SWA_KV_HEADS):
            vv = v_ref[0, pl.ds(ks, 2 * W), kv * LANES:(kv + 1) * LANES]
            vv = jnp.where(krow == sink_at, jnp.zeros_like(vv), vv)
            outs.append(_dot(pb[kv * grp * W:(kv + 1) * grp * W], jnp.concatenate([vv, ones], axis=1)))
        o = jnp.concatenate(outs, axis=0)
        o = o[:, :LANES] / o[:, LANES:]
        for pr in range(SWA_Q_HEADS // 2):
            o_pair = jnp.where(lane < SWA_HD, o[2 * pr * W:(2 * pr + 1) * W], o[(2 * pr + 1) * W:(2 * pr + 2) * W])
            o_ref[0, rows, pr * LANES:(pr + 1) * LANES] = o_pair.astype(o_ref.dtype)


def _swa(q, k, v, pos, sinks):
    b, s, _ = q.shape
    tq = min(SWA_TILE, s)
    W = SWA_WINDOW
    nb = s // W
    pos_col = pos.reshape(b, s, 1)
    pos_row = jnp.concatenate([pos.reshape(b, nb, 1, W), jnp.zeros((b, 1, 1, W), pos.dtype)], axis=1)
    return pl.pallas_call(
        functools.partial(_swa_body, tq=tq),
        out_shape=jax.ShapeDtypeStruct((b, s, SWA_Q_HEADS * SWA_HD), BF16),
        grid=(b, s // tq),
        in_specs=[
            pl.BlockSpec((1, tq, SWA_Q_HEADS * LANES), lambda bb, i: (bb, i, 0)),
            pl.BlockSpec((1, s, SWA_KV_HEADS * LANES), lambda bb, i: (bb, 0, 0)),
            pl.BlockSpec((1, s, SWA_KV_HEADS * LANES), lambda bb, i: (bb, 0, 0)),
            pl.BlockSpec((1, tq, 1), lambda bb, i: (bb, i, 0)),
            pl.BlockSpec((1, nb + 1, 1, W), lambda bb, i: (bb, 0, 0, 0)),
            _resident((1, SWA_Q_HEADS)),
        ],
        out_specs=pl.BlockSpec((1, tq, SWA_Q_HEADS * SWA_HD), lambda bb, i: (bb, i, 0)),
        compiler_params=_params("parallel", "parallel"),
        name="swa_sink_attention",
    )(q, k, v, pos_col, pos_row, sinks.reshape(1, SWA_Q_HEADS))


def _log_sigmoid(x):
    return jnp.minimum(x, 0.0) - jnp.log(1.0 + jnp.exp(-jnp.abs(x)))


def _gla_body(q_ref, k_ref, v_ref, ga_ref, gr_ref, wgb_ref, gb_ref, nrm_ref, o_ref, st_ref, *, tg):
    C = GLA_CHUNK

    @pl.when(pl.program_id(1) == 0)
    def _():
        st_ref[...] = jnp.zeros_like(st_ref)

    row = lax.broadcasted_iota(jnp.int32, (tg, tg), 0)
    col = lax.broadcasted_iota(jnp.int32, (tg, tg), 1)
    shift = C.bit_length() - 1
    same_chunk = lax.shift_right_logical(row, shift) == lax.shift_right_logical(col, shift)
    intra = same_chunk & (col <= row)
    hw = GLA_HEADS * LANES
    g = _log_sigmoid(_dot(ga_ref[0], wgb_ref[...]) + gb_ref[...]) * (math.log2(math.e) / GLA_TAU)
    sums = _dot(jnp.where(intra, 1.0, 0.0).astype(BF16), jnp.concatenate(_split_bf16(g, 2), axis=1))
    bcum = sums[:, :hw] + sums[:, hw:]
    btot = jnp.concatenate([jnp.broadcast_to(bcum[c * C + C - 1:(c + 1) * C, :], (C, hw))
                            for c in range(tg // C)], axis=0)
    q_dec = (q_ref[0].astype(F32) * jnp.exp2(bcum)).astype(BF16)
    kf = k_ref[0].astype(F32)
    k_inv = (kf * jnp.exp2(-bcum)).astype(BF16)
    k_end = (kf * jnp.exp2(btot - bcum)).astype(BF16)
    chunk_dec = jnp.exp2(btot)
    n_c = tg // C
    heads = [slice(hd * LANES, (hd + 1) * LANES) for hd in range(GLA_HEADS)]
    chunks = [slice(c * C, (c + 1) * C) for c in range(n_c)]
    vs = [v_ref[0, :, sl] for sl in heads]
    o_intra = [_dot(jnp.where(intra, _dot_nt(q_dec[:, sl], k_inv[:, sl]), 0.0).astype(BF16), vh)
               for sl, vh in zip(heads, vs)]
    kvs = [[_dot_tn(vh[rows], k_end[rows, sl]) for rows in chunks] for sl, vh in zip(heads, vs)]
    for hd, sl in enumerate(heads):
        st = st_ref[hd]
        inter = []
        for c, rows in enumerate(chunks):
            inter.append(_dot_nt(q_dec[rows, sl], st.astype(BF16)))
            st = st * chunk_dec[c * C:c * C + 1, sl] + kvs[hd][c]
        st_ref[hd] = st
        o = o_intra[hd] + jnp.concatenate(inter, axis=0)
        o = _rms(o, nrm_ref[...]) * _silu(gr_ref[0, :, sl].astype(F32))
        o_ref[0, :, sl] = o.astype(o_ref.dtype)


def _gla(q, k, v, ga, gr, wgb, gate_bias, gla_norm):
    b, s, _ = q.shape
    tg = min(GLA_TILE, s)
    hw = GLA_HEADS * LANES
    blk = lambda n: pl.BlockSpec((1, tg, n), lambda bb, i: (bb, i, 0))
    return pl.pallas_call(
        functools.partial(_gla_body, tg=tg),
        out_shape=jax.ShapeDtypeStruct((b, s, GLA_HEADS * GLA_DV), BF16),
        grid=(b, s // tg),
        in_specs=[blk(hw), blk(hw), blk(GLA_HEADS * GLA_DV), blk(LANES), blk(GLA_HEADS * GLA_DV),
                  _resident(wgb.shape), _resident((1, hw)), _resident((1, GLA_DV))],
        out_specs=blk(GLA_HEADS * GLA_DV),
        scratch_shapes=[pltpu.VMEM((GLA_HEADS, GLA_DV, LANES), F32)],
        compiler_params=_params("parallel", "arbitrary"),
        name="gla_chunked",
    )(q, k, v, ga, gr, wgb, gate_bias, gla_norm.reshape(1, GLA_DV))


def _pad_heads(w, n_heads, width, lanes=LANES, offset=0):
    lead = w.shape[:-1]
    w = w.reshape(lead + (n_heads, width))
    cfg = [(0, 0)] * len(lead) + [(0, 0), (offset, lanes - width - offset)]
    return jnp.pad(w, cfg).reshape(lead + (n_heads * lanes,))


def _rot_cols(w):
    half = MLA_ROPE // 2
    lane = np.arange(LANES)
    src = np.where((lane >= MLA_NOPE) & (lane < MLA_NOPE + half), lane + half,
                   np.where((lane >= MLA_NOPE + half) & (lane < MLA_QK), lane - half, 0))
    keep = jnp.asarray((lane >= MLA_NOPE) & (lane < MLA_QK), w.dtype)
    blocks = w.reshape(w.shape[:-1] + (w.shape[-1] // LANES, LANES))
    return (blocks[..., src] * keep).reshape(w.shape)


def _pad_cols(w, total, offset=0):
    return jnp.pad(w, [(0, 0)] * (w.ndim - 1) + [(offset, total - w.shape[-1] - offset)])


def _even_mixer(x2, b, s, pos, mix_norm, w_in, conv_w, conv_b, dt_bias, a_log, d_skip, ssm_norm,
                q_a_norm, w_q_b, kv_a_norm, w_kv_b, q_norm, k_norm, w_out):
    t = b * s
    o0 = 0
    cols = {}
    for name, n in (("z", SSM_D), ("xbc", SSM_XBC), ("dt", SSM_HEADS), ("qa", MLA_Q_RANK),
                    ("kvc", MLA_KV_RANK), ("kpe", MLA_ROPE)):
        cols[name] = w_in[:, o0:o0 + n]
        o0 += n
    w_kpe = _pad_cols(cols["kpe"], LANES, offset=MLA_NOPE)
    weights = [cols["z"], cols["xbc"], _pad_cols(cols["dt"], LANES), cols["qa"], cols["kvc"],
               jnp.concatenate([w_kpe, _rot_cols(w_kpe)], axis=1)]
    weights = [w.astype(BF16) for w in weights]
    cos_full, sin_full = _rope_tables(pos)
    wq = _pad_heads(w_q_b, MLA_HEADS, MLA_QK)
    wq = jnp.concatenate([wq, _rot_cols(wq)], axis=1).astype(BF16)
    with_rot = lambda g: jnp.concatenate([g, _rot_cols(g)], axis=1)
    w_kv = w_kv_b.reshape(MLA_KV_RANK, MLA_HEADS, MLA_NOPE + MLA_V)
    wk = _pad_heads(w_kv[:, :, :MLA_NOPE].reshape(MLA_KV_RANK, -1), MLA_HEADS, MLA_NOPE).astype(BF16)
    wv = w_kv[:, :, MLA_NOPE:].reshape(MLA_KV_RANK, MLA_HEADS * MLA_V).astype(BF16)
    z, xbc, dt, q, k, v = _even_inproj(x2, mix_norm, weights, cos_full, sin_full, q_a_norm.reshape(1, -1),
                                       kv_a_norm.reshape(1, -1), wq, wk, wv,
                                       with_rot(_pad_cols(q_norm.reshape(1, -1), LANES)),
                                       with_rot(_pad_cols(k_norm.reshape(1, -1), LANES)))
    y = _ssd(xbc.reshape(b, s, -1), z.reshape(b, s, -1), dt.reshape(b, s, -1), conv_w, conv_b,
             dt_bias, a_log, d_skip, ssm_norm).reshape(t, SSM_D)
    o = _mla_attention(q.reshape(b, s, -1), k.reshape(b, s, -1), v.reshape(b, s, -1)).reshape(t, -1)
    return [y, o], [w_out[:SSM_D], w_out[SSM_D:]]


def _odd_mixer(x2, b, s, pos, mix_norm, w_in, q_norm, k_norm, sinks, w_gate_b, gate_bias, gla_norm,
               w_out):
    t = b * s
    sizes = [SWA_Q_HEADS * SWA_HD, SWA_KV_HEADS * SWA_HD, SWA_KV_HEADS * SWA_HD,
             GLA_HEADS * GLA_DK, GLA_HEADS * GLA_DK, GLA_HEADS * GLA_DV, GLA_RANK, GLA_HEADS * GLA_DV]
    parts, o0 = [], 0
    for n in sizes:
        parts.append(w_in[:, o0:o0 + n])
        o0 += n
    wq, wk, wv, wgq, wgk, wgv, wga, wgr = parts
    v_dup = jnp.repeat(wv.reshape(D_MODEL, SWA_KV_HEADS, 1, SWA_HD), 2, axis=2).reshape(D_MODEL, -1)
    weights = [_pad_heads(wq, SWA_Q_HEADS, SWA_HD), _pad_heads(wk, SWA_KV_HEADS, SWA_HD), v_dup,
               _pad_heads(wgq * (GLA_DK ** -0.5), GLA_HEADS, GLA_DK), _pad_heads(wgk, GLA_HEADS, GLA_DK),
               wgv, _pad_cols(wga, LANES), wgr]
    weights = [w.astype(BF16) for w in weights]
    gains = [_pad_cols(q_norm.reshape(1, -1) * (SWA_HD ** -0.5 * math.log2(math.e)), LANES),
             _pad_cols(k_norm.reshape(1, -1), LANES)]
    q, k, v, gq, gk, gv, ga, gr = _inproj(x2, mix_norm, weights, [BF16] * 8,
                                          [SWA_HD, SWA_HD, 0, 0, 0, 0, 0, 0], gains)
    o_swa = _swa(q.reshape(b, s, -1), k.reshape(b, s, -1), v.reshape(b, s, -1), pos, sinks)
    wgb = _pad_heads(jnp.pad(w_gate_b, ((0, LANES - GLA_RANK), (0, 0))), GLA_HEADS, GLA_DK).astype(BF16)
    gb = _pad_heads(gate_bias.reshape(1, -1), GLA_HEADS, GLA_DK)
    o_gla = _gla(gq.reshape(b, s, -1), gk.reshape(b, s, -1), gv.reshape(b, s, -1), ga.reshape(b, s, -1),
                 gr.reshape(b, s, -1), wgb, gb, gla_norm)
    n_swa = SWA_Q_HEADS * SWA_HD
    return [o_swa.reshape(t, -1), o_gla.reshape(t, -1)], [w_out[:n_swa], w_out[n_swa:]]


def kernel(x, positions, pre_norm, pre_w_gate, pre_w_up, pre_w_down, mix_norm, post_norm, post_w_gate,
           post_w_up, post_w_down, e_w_in, e_conv_w, e_conv_b, e_dt_bias, e_a_log, e_d_skip, e_ssm_norm,
           e_q_a_norm, e_w_q_b, e_kv_a_norm, e_w_kv_b, e_q_norm, e_k_norm, e_w_out, o_w_in, o_q_norm,
           o_k_norm, o_sinks, o_w_gate_b, o_gate_bias, o_gla_norm, o_w_out):
    b, s, d = x.shape
    depth = pre_norm.shape[0]
    x2 = x.reshape(b * s, d)
    for layer in range(depth):
        x2 = _ffn(x2, pre_norm[layer], pre_w_gate[layer], pre_w_up[layer], pre_w_down[layer])
        j = layer // 2
        if layer % 2 == 0:
            acts, w_outs = _even_mixer(x2, b, s, positions, mix_norm[layer], e_w_in[j], e_conv_w[j],
                                       e_conv_b[j], e_dt_bias[j], e_a_log[j], e_d_skip[j], e_ssm_norm[j],
                                       e_q_a_norm[j], e_w_q_b[j], e_kv_a_norm[j], e_w_kv_b[j], e_q_norm[j],
                                       e_k_norm[j], e_w_out[j])
        else:
            acts, w_outs = _odd_mixer(x2, b, s, positions, mix_norm[layer], o_w_in[j], o_q_norm[j],
                                      o_k_norm[j], o_sinks[j], o_w_gate_b[j], o_gate_bias[j],
                                      o_gla_norm[j], o_w_out[j])
        x2 = _ffn(x2, post_norm[layer], post_w_gate[layer], post_w_up[layer], post_w_down[layer],
                  acts, w_outs)
    return x2.reshape(b, s, d)
```

```python
import functools
import math

import jax
import jax.numpy as jnp
import numpy as np
from jax import lax
from jax.experimental import pallas as pl
from jax.experimental.pallas import tpu as pltpu

F32 = jnp.float32
BF16 = jnp.bfloat16

D_MODEL = 1024
D_FF = 2816
FFN_RES = 0.5
EPS = 1e-6
SSM_HEADS = 16
SSM_HEAD_DIM = 64
SSM_D = SSM_HEADS * SSM_HEAD_DIM
SSM_GROUPS = 4
SSM_STATE = 128
SSM_CONV = 4
SSM_CHUNK = 128
SSM_XBC = SSM_D + 2 * SSM_GROUPS * SSM_STATE
MLA_HEADS = 8
MLA_Q_RANK = 384
MLA_KV_RANK = 256
MLA_NOPE = 64
MLA_ROPE = 32
MLA_QK = MLA_NOPE + MLA_ROPE
MLA_V = 64
ROPE_THETA = 10000.0
SWA_Q_HEADS = 8
SWA_KV_HEADS = 2
SWA_HD = 64
SWA_WINDOW = 128
GLA_HEADS = 4
GLA_DK = 64
GLA_DV = 128
GLA_RANK = 16
GLA_TAU = 16.0
GLA_CHUNK = 64

LANES = 128
VMEM_LIMIT_BYTES = 56 * 1024 * 1024

FF_CHUNK = 256
TOKEN_TILE = 1024
FFN_TILE = 1024
MLA_BLOCK = 512
SWA_TILE = 1024
GLA_TILE = 256
SSD_HISTORY = 16
SSD_WINDOW = 256
SSD_CHUNKS_PER_STEP = 4


def _params(*sem):
    return pltpu.CompilerParams(dimension_semantics=sem, vmem_limit_bytes=VMEM_LIMIT_BYTES)


def _resident(shape):
    nd = len(shape)
    return pl.BlockSpec(shape, lambda *_: (0,) * nd, pipeline_mode=pl.Buffered(1))


def _rms(x, gain):
    return x * lax.rsqrt(jnp.mean(x * x, axis=-1, keepdims=True) + EPS) * gain


def _silu(x):
    return x * jax.nn.sigmoid(x)


def _dot(a, b):
    return jnp.dot(a, b, preferred_element_type=F32)


def _dot_nt(a, b):
    return lax.dot_general(a, b, (((1,), (1,)), ((), ())), preferred_element_type=F32)


def _split_bf16(x, n):
    parts = []
    for _ in range(n - 1):
        p = x.astype(BF16)
        parts.append(p)
        x = x - p.astype(F32)
    parts.append(x.astype(BF16))
    return parts


def _dot_tn(a, b):
    return lax.dot_general(a, b, (((0,), (0,)), ((), ())), preferred_element_type=F32)


def _ffn_body(x_ref, g_ref, wg_ref, wu_ref, wd_ref, *refs, n_mix):
    mix_a, mix_w = refs[:n_mix], refs[n_mix:2 * n_mix]
    o_ref, h_ref, acc_ref = refs[2 * n_mix:]
    x = x_ref[...]
    if n_mix:
        mix = _dot(mix_a[0][...], mix_w[0][...])
        for a_ref, w_ref in zip(mix_a[1:], mix_w[1:]):
            mix = mix + _dot(a_ref[...], w_ref[...])
        x = x + mix
    acc_ref[...] = x
    h_ref[...] = _rms(acc_ref[...], g_ref[...]).astype(BF16)
    for c in range(D_FF // FF_CHUNK):
        cols = slice(c * FF_CHUNK, (c + 1) * FF_CHUNK)
        gate = _dot(h_ref[...], wg_ref[:, cols])
        up = _dot(h_ref[...], wu_ref[:, cols])
        acc_ref[...] += _dot((_silu(gate) * up).astype(BF16), wd_ref[cols, :])
    o_ref[...] = acc_ref[...]


def _ffn(x2, norm, w_gate, w_up, w_down, mix_acts=(), mix_weights=()):
    t = x2.shape[0]
    tm = min(FFN_TILE, t)
    weights = [w_gate.astype(BF16), w_up.astype(BF16), (FFN_RES * w_down).astype(BF16)]
    mix_weights = [w.astype(BF16) for w in mix_weights]
    row = lambda n: pl.BlockSpec((tm, n), lambda i: (i, 0))
    return pl.pallas_call(
        functools.partial(_ffn_body, n_mix=len(mix_acts)),
        out_shape=jax.ShapeDtypeStruct((t, D_MODEL), F32),
        grid=(t // tm,),
        in_specs=[row(D_MODEL), _resident((1, D_MODEL))] + [_resident(w.shape) for w in weights]
        + [row(a.shape[1]) for a in mix_acts] + [_resident(w.shape) for w in mix_weights],
        out_specs=row(D_MODEL),
        scratch_shapes=[pltpu.VMEM((tm, D_MODEL), BF16), pltpu.VMEM((tm, D_MODEL), F32)],
        compiler_params=_params("parallel"),
        name="swiglu_half_step",
    )(x2, norm.reshape(1, D_MODEL), *weights, *mix_acts, *mix_weights)


def _head_norm(t, gain, n_valid):
    ms = jnp.sum(t * t, axis=-1, keepdims=True) * (1.0 / n_valid)
    return t * lax.rsqrt(ms + EPS) * gain


def _inproj_body(x_ref, g_ref, *refs, head_norms):
    n_out = len(head_norms)
    w_refs = refs[:n_out]
    n_gain = sum(1 for hn in head_norms if hn)
    gain_refs = refs[n_out:n_out + n_gain]
    o_refs = refs[n_out + n_gain:]
    h = _rms(x_ref[...], g_ref[...]).astype(BF16)
    gi = 0
    for w_ref, o_ref, hn in zip(w_refs, o_refs, head_norms):
        y = _dot(h, w_ref[...])
        if hn:
            gain = gain_refs[gi][...]
            gi += 1
            for hd in range(y.shape[1] // LANES):
                sl = slice(hd * LANES, (hd + 1) * LANES)
                o_ref[:, sl] = _head_norm(y[:, sl], gain, hn).astype(o_ref.dtype)
        else:
            o_ref[...] = y.astype(o_ref.dtype)


def _inproj(x2, norm, weights, out_dtypes, head_norms, gains):
    t = x2.shape[0]
    tm = min(TOKEN_TILE, t)
    in_specs = [pl.BlockSpec((tm, D_MODEL), lambda i: (i, 0)), _resident((1, D_MODEL))]
    in_specs += [_resident(w.shape) for w in weights]
    in_specs += [_resident(g.shape) for g in gains]
    return pl.pallas_call(
        functools.partial(_inproj_body, head_norms=tuple(head_norms)),
        out_shape=[jax.ShapeDtypeStruct((t, w.shape[1]), dt) for w, dt in zip(weights, out_dtypes)],
        grid=(t // tm,),
        in_specs=in_specs,
        out_specs=[pl.BlockSpec((tm, w.shape[1]), lambda i: (i, 0)) for w in weights],
        compiler_params=_params("parallel"),
        name="mixer_in_proj",
    )(x2, norm.reshape(1, D_MODEL), *weights, *gains)


def _rope_body(pos_ref, inv_ref, spread_ref, base_ref, cos_ref, sin_ref):
    ang = pos_ref[...] * inv_ref[...]
    terms = _split_bf16(jnp.cos(ang), 3) + _split_bf16(jnp.sin(ang), 3)
    lhs = jnp.concatenate(terms, axis=1)
    per_row = LANES // (MLA_ROPE // 2)
    tr = lhs.shape[0]
    for p in range(per_row):
        wide = _dot(lhs, spread_ref[p])
        cos_ref[pl.ds(p, tr, stride=per_row), :] = wide[:, :LANES] + base_ref[...]
        sin_ref[pl.ds(p, tr, stride=per_row), :] = wide[:, LANES:]


def _rope_tables(positions):
    b, s = positions.shape
    half = MLA_ROPE // 2
    inv = ROPE_THETA ** (-jnp.arange(0, MLA_ROPE, 2, dtype=F32) / MLA_ROPE)
    per_row = LANES // half
    rows = b * s // per_row
    pos_rep = jnp.repeat(positions.astype(F32).reshape(rows, per_row), half, axis=1)
    inv_row = jnp.tile(inv, per_row).reshape(1, LANES)
    n_terms = 3
    spread = np.zeros((per_row, 2 * n_terms * LANES, 2 * LANES), np.float32)
    f = np.arange(half)
    for p in range(per_row):
        for part in range(n_terms):
            src_c = part * LANES + p * half + f
            src_s = (n_terms + part) * LANES + p * half + f
            spread[p, src_c, MLA_NOPE + f] = 1.0
            spread[p, src_c, MLA_NOPE + half + f] = 1.0
            spread[p, src_s, LANES + MLA_NOPE + f] = -1.0
            spread[p, src_s, LANES + MLA_NOPE + half + f] = 1.0
    lane = np.arange(LANES)
    base = np.where((lane >= MLA_NOPE) & (lane < MLA_QK), 0.0, 1.0).astype(np.float32).reshape(1, LANES)
    tr = min(1024, rows)
    return pl.pallas_call(
        _rope_body,
        out_shape=[jax.ShapeDtypeStruct((b * s, LANES), F32)] * 2,
        grid=(rows // tr,),
        in_specs=[pl.BlockSpec((tr, LANES), lambda i: (i, 0)), _resident((1, LANES)),
                  _resident(spread.shape), _resident((1, LANES))],
        out_specs=[pl.BlockSpec((tr * per_row, LANES), lambda i: (i, 0))] * 2,
        compiler_params=_params("parallel"),
        name="rope_tables",
    )(pos_rep, inv_row, jnp.asarray(spread, BF16), jnp.asarray(base))


def _even_inproj_body(x_ref, g_ref, wz_ref, wxbc_ref, wdt_ref, wqa_ref, wkvc_ref, wkpe_ref, cos_ref, sin_ref,
                      qan_ref, kvn_ref, wq_ref, wk_ref, wv_ref, qn_ref, kn_ref,
                      z_ref, xbc_ref, dt_ref, q_ref, k_ref, v_ref):
    h = _rms(x_ref[...], g_ref[...]).astype(BF16)
    z_ref[...] = _dot(h, wz_ref[...]).astype(z_ref.dtype)
    xbc_ref[...] = _dot(h, wxbc_ref[...]).astype(xbc_ref.dtype)
    dt_ref[...] = _dot(h, wdt_ref[...])
    qa = _rms(_dot(h, wqa_ref[...]), qan_ref[...]).astype(BF16)
    kvc = _rms(_dot(h, wkvc_ref[...]), kvn_ref[...]).astype(BF16)
    kpe2 = _dot(h, wkpe_ref[...])

    hw = MLA_HEADS * LANES
    cosf = cos_ref[...]
    sinf = sin_ref[...]
    qscale = MLA_QK ** -0.5 * math.log2(math.e)
    q_cos = (qn_ref[:, :LANES] * qscale) * cosf
    q_sin = (qn_ref[:, LANES:] * qscale) * sinf
    k_cos = kn_ref[:, :LANES] * cosf
    kpe = kpe2[:, :LANES]
    k_rot = kpe2[:, LANES:] * (kn_ref[:, LANES:] * sinf)

    def inv_rms(t):
        return lax.rsqrt(jnp.sum(t * t, axis=-1, keepdims=True) * (1.0 / MLA_QK) + EPS)

    q_all = _dot(qa, wq_ref[...])
    k_all = _dot(kvc, wk_ref[...])
    v_ref[...] = _dot(kvc, wv_ref[...]).astype(v_ref.dtype)
    for hd in range(MLA_HEADS):
        sl = slice(hd * LANES, (hd + 1) * LANES)
        yq = q_all[:, sl]
        yq_rot = q_all[:, hw + hd * LANES:hw + (hd + 1) * LANES]
        q_ref[:, sl] = ((yq * q_cos + yq_rot * q_sin) * inv_rms(yq)).astype(q_ref.dtype)
        yk = k_all[:, sl] + kpe
        k_ref[:, sl] = ((yk * k_cos + k_rot) * inv_rms(yk)).astype(k_ref.dtype)


def _even_inproj(x2, norm, in_weights, cos_full, sin_full, qan, kvn, wq, wk, wv, qn, kn):
    t = x2.shape[0]
    tm = min(TOKEN_TILE, t)
    row = lambda n: pl.BlockSpec((tm, n), lambda i: (i, 0))
    hw = MLA_HEADS * LANES
    consts = [qan, kvn, wq, wk, wv, qn, kn]
    return pl.pallas_call(
        _even_inproj_body,
        out_shape=[jax.ShapeDtypeStruct((t, SSM_D), BF16), jax.ShapeDtypeStruct((t, SSM_XBC), BF16),
                   jax.ShapeDtypeStruct((t, LANES), F32), jax.ShapeDtypeStruct((t, hw), BF16),
                   jax.ShapeDtypeStruct((t, hw), BF16), jax.ShapeDtypeStruct((t, MLA_HEADS * MLA_V), BF16)],
        grid=(t // tm,),
        in_specs=[row(D_MODEL), _resident((1, D_MODEL))] + [_resident(w.shape) for w in in_weights]
        + [row(LANES), row(LANES)] + [_resident(c.shape) for c in consts],
        out_specs=[row(SSM_D), row(SSM_XBC), row(LANES), row(hw), row(hw), row(MLA_HEADS * MLA_V)],
        compiler_params=_params("parallel"),
        name="even_in_proj_mla_prep",
    )(x2, norm.reshape(1, D_MODEL), *in_weights, cos_full, sin_full, *consts)


def _mla_attn_body(q_ref, k_ref, v_ref, o_ref, *, blk, n_blk):
    row = lax.broadcasted_iota(jnp.int32, (blk, blk), 0)
    col = lax.broadcasted_iota(jnp.int32, (blk, blk), 1)
    causal = col <= row
    lane = lax.broadcasted_iota(jnp.int32, (blk, LANES), 1)
    ones = jnp.ones((blk, LANES), BF16)
    for i in range(n_blk):
        rows = slice(i * blk, (i + 1) * blk)
        past = slice(0, i * blk)
        outs = []
        for hd in range(2):
            sl = slice(hd * LANES, (hd + 1) * LANES)
            q = q_ref[0, rows, sl]
            s_d = jnp.where(causal, _dot_nt(q, k_ref[0, rows, sl]), -jnp.inf)
            m = jnp.max(s_d, axis=-1, keepdims=True)
            if i:
                s_p = _dot_nt(q, k_ref[0, past, sl])
                m = jnp.maximum(m, jnp.max(s_p, axis=-1, keepdims=True))
            acc = _dot(jnp.exp2(s_d - m).astype(BF16), jnp.concatenate([v_ref[0, rows, :], ones], axis=1))
            if i:
                acc = acc + _dot(jnp.exp2(s_p - m).astype(BF16),
                                 jnp.concatenate([v_ref[0, past, :], jnp.ones((i * blk, LANES), BF16)], axis=1))
            outs.append(acc[:, :LANES] / acc[:, LANES:])
        o_ref[0, rows, :] = jnp.where(lane < MLA_V, outs[0], outs[1]).astype(o_ref.dtype)


def _mla_attention(q, k, v):
    b, s, _ = q.shape
    blk = min(MLA_BLOCK, s)
    pairs = MLA_HEADS // 2
    return pl.pallas_call(
        functools.partial(_mla_attn_body, blk=blk, n_blk=s // blk),
        out_shape=jax.ShapeDtypeStruct((b, s, MLA_HEADS * MLA_V), BF16),
        grid=(b, pairs),
        in_specs=[
            pl.BlockSpec((1, s, 2 * LANES), lambda bb, p: (bb, 0, p)),
            pl.BlockSpec((1, s, 2 * LANES), lambda bb, p: (bb, 0, p)),
            pl.BlockSpec((1, s, LANES), lambda bb, p: (bb, 0, p)),
        ],
        out_specs=pl.BlockSpec((1, s, LANES), lambda bb, p: (bb, 0, p)),
        compiler_params=_params("parallel", "parallel"),
        name="mla_causal_attention",
    )(q, k, v)


def _softplus(x):
    return jnp.maximum(x, 0.0) + jnp.log(1.0 + jnp.exp(-jnp.abs(x)))


def _ssd_body(xbc_ref, z_ref, dt_ref, cw_ref, cb_ref, dtb_ref, alog_ref, dsk_ref, nrm_ref,
              e64_ref, shift_ref, y_ref, xpad_ref, st_ref, *, n_chunks):
    L = SSM_CHUNK
    pad = SSD_HISTORY

    @pl.when(pl.program_id(1) == 0)
    def _():
        xpad_ref[...] = jnp.zeros_like(xpad_ref)
        st_ref[...] = jnp.zeros_like(st_ref)

    xpad_ref[pad:pad + n_chunks * L, :] = xbc_ref[0]
    convs = []
    for ci in range(n_chunks):
        taps = _dot(shift_ref[L:, :], xpad_ref[ci * L:ci * L + SSD_WINDOW, :])
        conv = cb_ref[...] + cw_ref[SSM_CONV - 1:SSM_CONV, :] * xbc_ref[0, ci * L:(ci + 1) * L, :].astype(F32)
        for back in range(1, SSM_CONV):
            tap = SSM_CONV - 1 - back
            conv = conv + cw_ref[tap:tap + 1, :] * taps[(back - 1) * L:back * L]
        convs.append(conv)
    xpad_ref[0:pad, :] = xpad_ref[n_chunks * L:n_chunks * L + pad, :]
    for ci in range(n_chunks):
        _ssd_chunk(convs[ci], slice(ci * L, (ci + 1) * L), z_ref, dt_ref, dtb_ref, alog_ref, dsk_ref,
                   nrm_ref, e64_ref, y_ref, st_ref)


def _ssd_chunk(conv, rows, z_ref, dt_ref, dtb_ref, alog_ref, dsk_ref, nrm_ref, e64_ref, y_ref, st_ref):
    L = SSM_CHUNK
    act = _silu(conv)
    xs = act[:, :SSM_D]
    gn = SSM_GROUPS * SSM_STATE
    b_all = act[:, SSM_D:SSM_D + gn]
    c_all = act[:, SSM_D + gn:]

    lane = lax.broadcasted_iota(jnp.int32, (L, LANES), 1)
    row = lax.broadcasted_iota(jnp.int32, (L, L), 0)
    col = lax.broadcasted_iota(jnp.int32, (L, L), 1)
    dt = jnp.where(lane < SSM_HEADS, _softplus(dt_ref[0, rows] + dtb_ref[...]), 0.0)
    a = dt * (-math.log2(math.e) * jnp.exp(alog_ref[...]))
    a_parts = _split_bf16(a, 3)
    tri = jnp.where(col <= row, 1.0, 0.0).astype(BF16)
    cs3 = _dot(tri, jnp.concatenate(a_parts, axis=1))
    cs = cs3[:, :LANES] + cs3[:, LANES:2 * LANES] + cs3[:, 2 * LANES:]
    cs_t = cs.T
    cs_last = cs[L - 1:L, :]
    ecs = jnp.exp2(cs)
    stacked = jnp.concatenate([dt, ecs, jnp.exp2(cs_last - cs)], axis=0)
    wide = _dot(jnp.concatenate(_split_bf16(stacked, 2), axis=1), e64_ref[...])
    dt_w, ecs_w, dec_w = wide[:L], wide[L:2 * L], wide[2 * L:]

    xd = xs * dt_w
    xd_b = xd.astype(BF16)
    xdw_b = (xd * dec_w).astype(BF16)
    chunk_decay = ecs_w[L - 1:L, :]

    heads_per_group = SSM_HEADS // SSM_GROUPS
    gw = heads_per_group * SSM_HEAD_DIM
    cs_col = jnp.concatenate([jnp.broadcast_to(cs[:, h:h + 1], (L, L)) for h in range(SSM_HEADS)], axis=1)
    cs_row = jnp.concatenate([jnp.broadcast_to(cs_t[h:h + 1, :], (L, L)) for h in range(SSM_HEADS)], axis=1)
    decay = jnp.exp2(jnp.minimum(cs_col - cs_row, 0.0))
    b_bf = b_all.astype(BF16)
    c_bf = c_all.astype(BF16)
    cbs = [_dot_nt(c_bf[:, g * SSM_STATE:(g + 1) * SSM_STATE], b_bf[:, g * SSM_STATE:(g + 1) * SSM_STATE])
           for g in range(SSM_GROUPS)]
    cbs = [jnp.where(col <= row, cb, 0.0) for cb in cbs]
    cb_all = jnp.concatenate([cbs[h // heads_per_group] for h in range(SSM_HEADS)], axis=1)
    mix = (cb_all * decay).astype(BF16)

    y_parts = []
    for pi in range(SSM_HEADS // 2):
        pair = xd_b[:, pi * LANES:(pi + 1) * LANES]
        zero = jnp.zeros_like(pair)
        rhs = jnp.concatenate([jnp.where(lane < SSM_HEAD_DIM, pair, zero),
                               jnp.where(lane < SSM_HEAD_DIM, zero, pair)], axis=0)
        y_parts.append(_dot(mix[:, 2 * pi * L:(2 * pi + 2) * L], rhs))
    yoff_parts = []
    for g in range(SSM_GROUPS):
        st_g = st_ref[:, g * gw:(g + 1) * gw]
        yoff_parts.append(_dot(c_bf[:, g * SSM_STATE:(g + 1) * SSM_STATE], st_g.astype(BF16)))
        bg_t = b_all[:, g * SSM_STATE:(g + 1) * SSM_STATE].T.astype(BF16)
        upd = _dot(bg_t, xdw_b[:, g * gw:(g + 1) * gw])
        st_ref[:, g * gw:(g + 1) * gw] = st_g * chunk_decay[:, g * gw:(g + 1) * gw] + upd

    y = jnp.concatenate(y_parts, axis=1) + ecs_w * jnp.concatenate(yoff_parts, axis=1)
    y = (y + dsk_ref[...] * xs) * _silu(z_ref[0, rows].astype(F32))
    for g in range(SSM_GROUPS):
        sl = slice(g * gw, (g + 1) * gw)
        y_ref[0, rows, sl] = _rms(y[:, sl], nrm_ref[:, sl]).astype(y_ref.dtype)


def _ssd(xbc, z, dt, conv_w, conv_b, dt_bias, a_log, d_skip, ssm_norm):
    b, s, _ = xbc.shape
    L = SSM_CHUNK
    heads = np.arange(2 * LANES)[:, None] % LANES
    e64 = jnp.asarray(heads == (np.arange(SSM_D)[None, :] // SSM_HEAD_DIM), BF16)
    t_idx = np.arange(SSM_CONV * L)
    shift = np.zeros((SSM_CONV * L, SSD_WINDOW), np.float32)
    shift[t_idx, SSD_HISTORY + t_idx % L - t_idx // L] = 1.0
    shift = jnp.asarray(shift, BF16)
    padh = lambda v: jnp.pad(v.reshape(1, SSM_HEADS), ((0, 0), (0, LANES - SSM_HEADS)))
    dsk = jnp.repeat(d_skip, SSM_HEAD_DIM).reshape(1, SSM_D)
    nc = SSD_CHUNKS_PER_STEP
    blk = lambda n: pl.BlockSpec((1, nc * L, n), lambda bb, c: (bb, c, 0))
    return pl.pallas_call(
        functools.partial(_ssd_body, n_chunks=nc),
        out_shape=jax.ShapeDtypeStruct((b, s, SSM_D), BF16),
        grid=(b, s // (nc * L)),
        in_specs=[blk(SSM_XBC), blk(SSM_D), blk(LANES),
                  _resident((SSM_CONV, SSM_XBC)), _resident((1, SSM_XBC)), _resident((1, LANES)),
                  _resident((1, LANES)), _resident((1, SSM_D)), _resident((1, SSM_D)),
                  _resident(e64.shape), _resident(shift.shape)],
        out_specs=blk(SSM_D),
        scratch_shapes=[pltpu.VMEM(((nc - 1) * L + SSD_WINDOW, SSM_XBC), BF16), pltpu.VMEM((SSM_STATE, SSM_D), F32)],
        compiler_params=_params("parallel", "arbitrary"),
        name="ssd_scan",
    )(xbc, z, dt, conv_w, conv_b.reshape(1, SSM_XBC), padh(dt_bias), padh(a_log), dsk,
      ssm_norm.reshape(1, SSM_D), e64, shift)


def _swa_body(q_ref, k_ref, v_ref, pc_ref, pr_ref, sink_ref, o_ref, *, tq):
    W = SWA_WINDOW
    grp = SWA_Q_HEADS // SWA_KV_HEADS
    log2e = math.log2(math.e)
    slopes = [log2e * 2.0 ** (-8.0 * (h + 1) / SWA_Q_HEADS) for h in range(SWA_Q_HEADS)]
    i = pl.program_id(1)
    lane = lax.broadcasted_iota(jnp.int32, (W, LANES), 1)
    qi = lax.broadcasted_iota(jnp.int32, (W, 2 * W), 0)
    ki = lax.broadcasted_iota(jnp.int32, (W, 2 * W), 1)
    sinks = sink_ref[...] * log2e
    sink_wide = jnp.concatenate(
        [jnp.broadcast_to(sinks[0:1, h:h + 1], (W, 2 * W)) for h in range(SWA_Q_HEADS)], axis=0)
    krow = lax.broadcasted_iota(jnp.int32, (2 * W, LANES), 0)
    ones = jnp.ones((2 * W, LANES), BF16)
    for jb in range(tq // W):
        j = i * (tq // W) + jb
        kb0 = jnp.maximum(j - 1, 0)
        ks = pl.multiple_of(kb0 * W, W)
        rel = (j - kb0) * W + qi - ki
        valid = (rel >= 0) & (rel < W)
        pq = pc_ref[0, jb * W:(jb + 1) * W, :]
        pk = jnp.concatenate([pr_ref[0, kb0], pr_ref[0, kb0 + 1]], axis=1)
        dist = jnp.where(valid, jnp.abs(pq - pk).astype(F32), jnp.inf)
        bias = jnp.concatenate([slopes[h] * dist for h in range(SWA_Q_HEADS)], axis=0)
        rows = slice(jb * W, (jb + 1) * W)
        logits = []
        for kv in range(SWA_KV_HEADS):
            kk = k_ref[0, pl.ds(ks, 2 * W), kv * LANES:(kv + 1) * LANES]
            qs = jnp.concatenate(
                [q_ref[0, rows, (kv * grp + g) * LANES:(kv * grp + g + 1) * LANES] for g in range(grp)],
                axis=0)
            logits.append(_dot_nt(qs, kk))
        s = jnp.concatenate(logits, axis=0) - bias
        sink_at = jnp.where(j == 0, 2 * W - 1, 0)
        s = jnp.where(jnp.concatenate([ki == sink_at] * SWA_Q_HEADS, axis=0), sink_wide, s)
        m = jnp.max(s, axis=-1, keepdims=True)
        pb = jnp.exp2(s - m).astype(BF16)
        outs = []
        for kv in range(SWA_KV_HEADS):
            vv = v_ref[0, pl.ds(ks, 2 * W), kv * LANES:(kv + 1) * LANES]
            vv = jnp.where(krow == sink_at, jnp.zeros_like(vv), vv)
            outs.append(_dot(pb[kv * grp * W:(kv + 1) * grp * W], jnp.concatenate([vv, ones], axis=1)))
        o = jnp.concatenate(outs, axis=0)
        o = o[:, :LANES] / o[:, LANES:]
        for pr in range(SWA_Q_HEADS // 2):
            o_pair = jnp.where(lane < SWA_HD, o[2 * pr * W:(2 * pr + 1) * W], o[(2 * pr + 1) * W:(2 * pr + 2) * W])
            o_ref[0, rows, pr * LANES:(pr + 1) * LANES] = o_pair.astype(o_ref.dtype)


def _swa(q, k, v, pos, sinks):
    b, s, _ = q.shape
    tq = min(SWA_TILE, s)
    W = SWA_WINDOW
    nb = s // W
    pos_col = pos.reshape(b, s, 1)
    pos_row = jnp.concatenate([pos.reshape(b, nb, 1, W), jnp.zeros((b, 1, 1, W), pos.dtype)], axis=1)
    return pl.pallas_call(
        functools.partial(_swa_body, tq=tq),
        out_shape=jax.ShapeDtypeStruct((b, s, SWA_Q_HEADS * SWA_HD), BF16),
        grid=(b, s // tq),
        in_specs=[
            pl.BlockSpec((1, tq, SWA_Q_HEADS * LANES), lambda bb, i: (bb, i, 0)),
            pl.BlockSpec((1, s, SWA_KV_HEADS * LANES), lambda bb, i: (bb, 0, 0)),
            pl.BlockSpec((1, s, SWA_KV_HEADS * LANES), lambda bb, i: (bb, 0, 0)),
            pl.BlockSpec((1, tq, 1), lambda bb, i: (bb, i, 0)),
            pl.BlockSpec((1, nb + 1, 1, W), lambda bb, i: (bb, 0, 0, 0)),
            _resident((1, SWA_Q_HEADS)),
        ],
        out_specs=pl.BlockSpec((1, tq, SWA_Q_HEADS * SWA_HD), lambda bb, i: (bb, i, 0)),
        compiler_params=_params("parallel", "parallel"),
        name="swa_sink_attention",
    )(q, k, v, pos_col, pos_row, sinks.reshape(1, SWA_Q_HEADS))


def _log_sigmoid(x):
    return jnp.minimum(x, 0.0) - jnp.log(1.0 + jnp.exp(-jnp.abs(x)))


def _gla_body(q_ref, k_ref, v_ref, ga_ref, gr_ref, wgb_ref, gb_ref, nrm_ref, o_ref, st_ref, *, tg):
    C = GLA_CHUNK

    @pl.when(pl.program_id(1) == 0)
    def _():
        st_ref[...] = jnp.zeros_like(st_ref)

    row = lax.broadcasted_iota(jnp.int32, (tg, tg), 0)
    col = lax.broadcasted_iota(jnp.int32, (tg, tg), 1)
    shift = C.bit_length() - 1
    same_chunk = lax.shift_right_logical(row, shift) == lax.shift_right_logical(col, shift)
    intra = same_chunk & (col <= row)
    hw = GLA_HEADS * LANES
    g = _log_sigmoid(_dot(ga_ref[0], wgb_ref[...]) + gb_ref[...]) * (math.log2(math.e) / GLA_TAU)
    sums = _dot(jnp.where(intra, 1.0, 0.0).astype(BF16), jnp.concatenate(_split_bf16(g, 2), axis=1))
    bcum = sums[:, :hw] + sums[:, hw:]
    btot = jnp.concatenate([jnp.broadcast_to(bcum[c * C + C - 1:(c + 1) * C, :], (C, hw))
                            for c in range(tg // C)], axis=0)
    q_dec = (q_ref[0].astype(F32) * jnp.exp2(bcum)).astype(BF16)
    kf = k_ref[0].astype(F32)
    k_inv = (kf * jnp.exp2(-bcum)).astype(BF16)
    k_end = (kf * jnp.exp2(btot - bcum)).astype(BF16)
    chunk_dec = jnp.exp2(btot)
    n_c = tg // C
    heads = [slice(hd * LANES, (hd + 1) * LANES) for hd in range(GLA_HEADS)]
    chunks = [slice(c * C, (c + 1) * C) for c in range(n_c)]
    vs = [v_ref[0, :, sl] for sl in heads]
    o_intra = [_dot(jnp.where(intra, _dot_nt(q_dec[:, sl], k_inv[:, sl]), 0.0).astype(BF16), vh)
               for sl, vh in zip(heads, vs)]
    kvs = [[_dot_tn(vh[rows], k_end[rows, sl]) for rows in chunks] for sl, vh in zip(heads, vs)]
    for hd, sl in enumerate(heads):
        st = st_ref[hd]
        inter = []
        for c, rows in enumerate(chunks):
            inter.append(_dot_nt(q_dec[rows, sl], st.astype(BF16)))
            st = st * chunk_dec[c * C:c * C + 1, sl] + kvs[hd][c]
        st_ref[hd] = st
        o = o_intra[hd] + jnp.concatenate(inter, axis=0)
        o = _rms(o, nrm_ref[...]) * _silu(gr_ref[0, :, sl].astype(F32))
        o_ref[0, :, sl] = o.astype(o_ref.dtype)


def _gla(q, k, v, ga, gr, wgb, gate_bias, gla_norm):
    b, s, _ = q.shape
    tg = min(GLA_TILE, s)
    hw = GLA_HEADS * LANES
    blk = lambda n: pl.BlockSpec((1, tg, n), lambda bb, i: (bb, i, 0))
    return pl.pallas_call(
        functools.partial(_gla_body, tg=tg),
        out_shape=jax.ShapeDtypeStruct((b, s, GLA_HEADS * GLA_DV), BF16),
        grid=(b, s // tg),
        in_specs=[blk(hw), blk(hw), blk(GLA_HEADS * GLA_DV), blk(LANES), blk(GLA_HEADS * GLA_DV),
                  _resident(wgb.shape), _resident((1, hw)), _resident((1, GLA_DV))],
        out_specs=blk(GLA_HEADS * GLA_DV),
        scratch_shapes=[pltpu.VMEM((GLA_HEADS, GLA_DV, LANES), F32)],
        compiler_params=_params("parallel", "arbitrary"),
        name="gla_chunked",
    )(q, k, v, ga, gr, wgb, gate_bias, gla_norm.reshape(1, GLA_DV))


def _pad_heads(w, n_heads, width, lanes=LANES, offset=0):
    lead = w.shape[:-1]
    w = w.reshape(lead + (n_heads, width))
    cfg = [(0, 0)] * len(lead) + [(0, 0), (offset, lanes - width - offset)]
    return jnp.pad(w, cfg).reshape(lead + (n_heads * lanes,))


def _rot_cols(w):
    half = MLA_ROPE // 2
    lane = np.arange(LANES)
    src = np.where((lane >= MLA_NOPE) & (lane < MLA_NOPE + half), lane + half,
                   np.where((lane >= MLA_NOPE + half) & (lane < MLA_QK), lane - half, 0))
    keep = jnp.asarray((lane >= MLA_NOPE) & (lane < MLA_QK), w.dtype)
    blocks = w.reshape(w.shape[:-1] + (w.shape[-1] // LANES, LANES))
    return (blocks[..., src] * keep).reshape(w.shape)


def _pad_cols(w, total, offset=0):
    return jnp.pad(w, [(0, 0)] * (w.ndim - 1) + [(offset, total - w.shape[-1] - offset)])


def _even_mixer(x2, b, s, pos, mix_norm, w_in, conv_w, conv_b, dt_bias, a_log, d_skip, ssm_norm,
                q_a_norm, w_q_b, kv_a_norm, w_kv_b, q_norm, k_norm, w_out):
    t = b * s
    o0 = 0
    cols = {}
    for name, n in (("z", SSM_D), ("xbc", SSM_XBC), ("dt", SSM_HEADS), ("qa", MLA_Q_RANK),
                    ("kvc", MLA_KV_RANK), ("kpe", MLA_ROPE)):
        cols[name] = w_in[:, o0:o0 + n]
        o0 += n
    w_kpe = _pad_cols(cols["kpe"], LANES, offset=MLA_NOPE)
    weights = [cols["z"], cols["xbc"], _pad_cols(cols["dt"], LANES), cols["qa"], cols["kvc"],
               jnp.concatenate([w_kpe, _rot_cols(w_kpe)], axis=1)]
    weights = [w.astype(BF16) for w in weights]
    cos_full, sin_full = _rope_tables(pos)
    wq = _pad_heads(w_q_b, MLA_HEADS, MLA_QK)
    wq = jnp.concatenate([wq, _rot_cols(wq)], axis=1).astype(BF16)
    with_rot = lambda g: jnp.concatenate([g, _rot_cols(g)], axis=1)
    w_kv = w_kv_b.reshape(MLA_KV_RANK, MLA_HEADS, MLA_NOPE + MLA_V)
    wk = _pad_heads(w_kv[:, :, :MLA_NOPE].reshape(MLA_KV_RANK, -1), MLA_HEADS, MLA_NOPE).astype(BF16)
    wv = w_kv[:, :, MLA_NOPE:].reshape(MLA_KV_RANK, MLA_HEADS * MLA_V).astype(BF16)
    z, xbc, dt, q, k, v = _even_inproj(x2, mix_norm, weights, cos_full, sin_full, q_a_norm.reshape(1, -1),
                                       kv_a_norm.reshape(1, -1), wq, wk, wv,
                                       with_rot(_pad_cols(q_norm.reshape(1, -1), LANES)),
                                       with_rot(_pad_cols(k_norm.reshape(1, -1), LANES)))
    y = _ssd(xbc.reshape(b, s, -1), z.reshape(b, s, -1), dt.reshape(b, s, -1), conv_w, conv_b,
             dt_bias, a_log, d_skip, ssm_norm).reshape(t, SSM_D)
    o = _mla_attention(q.reshape(b, s, -1), k.reshape(b, s, -1), v.reshape(b, s, -1)).reshape(t, -1)
    return [y, o], [w_out[:SSM_D], w_out[SSM_D:]]


def _odd_mixer(x2, b, s, pos, mix_norm, w_in, q_norm, k_norm, sinks, w_gate_b, gate_bias, gla_norm,
               w_out):
    t = b * s
    sizes = [SWA_Q_HEADS * SWA_HD, SWA_KV_HEADS * SWA_HD, SWA_KV_HEADS * SWA_HD,
             GLA_HEADS * GLA_DK, GLA_HEADS * GLA_DK, GLA_HEADS * GLA_DV, GLA_RANK, GLA_HEADS * GLA_DV]
    parts, o0 = [], 0
    for n in sizes:
        parts.append(w_in[:, o0:o0 + n])
        o0 += n
    wq, wk, wv, wgq, wgk, wgv, wga, wgr = parts
    v_dup = jnp.repeat(wv.reshape(D_MODEL, SWA_KV_HEADS, 1, SWA_HD), 2, axis=2).reshape(D_MODEL, -1)
    weights = [_pad_heads(wq, SWA_Q_HEADS, SWA_HD), _pad_heads(wk, SWA_KV_HEADS, SWA_HD), v_dup,
               _pad_heads(wgq * (GLA_DK ** -0.5), GLA_HEADS, GLA_DK), _pad_heads(wgk, GLA_HEADS, GLA_DK),
               wgv, _pad_cols(wga, LANES), wgr]
    weights = [w.astype(BF16) for w in weights]
    gains = [_pad_cols(q_norm.reshape(1, -1) * (SWA_HD ** -0.5 * math.log2(math.e)), LANES),
             _pad_cols(k_norm.reshape(1, -1), LANES)]
    q, k, v, gq, gk, gv, ga, gr = _inproj(x2, mix_norm, weights, [BF16] * 8,
                                          [SWA_HD, SWA_HD, 0, 0, 0, 0, 0, 0], gains)
    o_swa = _swa(q.reshape(b, s, -1), k.reshape(b, s, -1), v.reshape(b, s, -1), pos, sinks)
    wgb = _pad_heads(jnp.pad(w_gate_b, ((0, LANES - GLA_RANK), (0, 0))), GLA_HEADS, GLA_DK).astype(BF16)
    gb = _pad_heads(gate_bias.reshape(1, -1), GLA_HEADS, GLA_DK)
    o_gla = _gla(gq.reshape(b, s, -1), gk.reshape(b, s, -1), gv.reshape(b, s, -1), ga.reshape(b, s, -1),
                 gr.reshape(b, s, -1), wgb, gb, gla_norm)
    n_swa = SWA_Q_HEADS * SWA_HD
    return [o_swa.reshape(t, -1), o_gla.reshape(t, -1)], [w_out[:n_swa], w_out[n_swa:]]


def kernel(x, positions, pre_norm, pre_w_gate, pre_w_up, pre_w_down, mix_norm, post_norm, post_w_gate,
           post_w_up, post_w_down, e_w_in, e_conv_w, e_conv_b, e_dt_bias, e_a_log, e_d_skip, e_ssm_norm,
           e_q_a_norm, e_w_q_b, e_kv_a_norm, e_w_kv_b, e_q_norm, e_k_norm, e_w_out, o_w_in, o_q_norm,
           o_k_norm, o_sinks, o_w_gate_b, o_gate_bias, o_gla_norm, o_w_out):
    b, s, d = x.shape
    depth = pre_norm.shape[0]
    x2 = x.reshape(b * s, d)
    for layer in range(depth):
        x2 = _ffn(x2, pre_norm[layer], pre_w_gate[layer], pre_w_up[layer], pre_w_down[layer])
        j = layer // 2
        if layer % 2 == 0:
            acts, w_outs = _even_mixer(x2, b, s, positions, mix_norm[layer], e_w_in[j], e_conv_w[j],
                                       e_conv_b[j], e_dt_bias[j], e_a_log[j], e_d_skip[j], e_ssm_norm[j],
                                       e_q_a_norm[j], e_w_q_b[j], e_kv_a_norm[j], e_w_kv_b[j], e_q_norm[j],
                                       e_k_norm[j], e_w_out[j])
        else:
            acts, w_outs = _odd_mixer(x2, b, s, positions, mix_norm[layer], o_w_in[j], o_q_norm[j],
                                      o_k_norm[j], o_sinks[j], o_w_gate_b[j], o_gate_bias[j],
                                      o_gla_norm[j], o_w_out[j])
        x2 = _ffn(x2, post_norm[layer], post_w_gate[layer], post_w_up[layer], post_w_down[layer],
                  acts, w_outs)
    return x2.reshape(b, s, d)
```
